```python
import jax, jax.numpy as jnp
from jax import lax
import numpy as np

D_MODEL = 1024
BATCH = 4
SEQ = 8192
DEPTH = 2

N_MIXERS = 2
RMS_EPS = 1e-6
ROPE_THETA = 10000.0
NEG_INF = -1e30

MOBA_HEADS = 8
MOBA_HEAD_DIM = 128
MOBA_BLOCK = 256
MOBA_TOPK = 3
MOBA_Q_CHUNK = 64

MLA_HEADS = 8
MLA_Q_RANK = 384
MLA_KV_RANK = 256
MLA_NOPE = 128
MLA_ROPE = 64
MLA_V = 128
MLA_Q_BLOCK = 128

N_GROUPS = 4
EXPERTS_PER_GROUP = 4
N_EXPERTS = N_GROUPS * EXPERTS_PER_GROUP
EXPERT_FF = 256
TOPK_IN_GROUP = 2

PLE_DIM = 256

N_MOBA_LAYERS = (DEPTH + 1) // 2
N_MLA_LAYERS = DEPTH // 2

kernel_name = "hybrid_moba_mla_hmoe_ple"


def rms_norm(x, g):
    xf = x.astype(jnp.float32)
    y = xf * lax.rsqrt(jnp.mean(xf * xf, axis=-1, keepdims=True) + RMS_EPS)
    return (y * g.astype(jnp.float32)).astype(x.dtype)


def rope(x, pos):
    half = x.shape[-1] // 2
    inv = ROPE_THETA ** (-jnp.arange(half, dtype=jnp.float32) / half)
    ang = pos[:, None] * inv[None, :]
    cos, sin = jnp.cos(ang), jnp.sin(ang)
    x1 = x[..., :half].astype(jnp.float32)
    x2 = x[..., half:].astype(jnp.float32)
    out = jnp.concatenate([x1 * cos - x2 * sin, x1 * sin + x2 * cos], axis=-1)
    return out.astype(x.dtype)


def moba_attention(h, w_qkv, w_o):
    B, S, _ = h.shape
    H, Dh, L, QC = MOBA_HEADS, MOBA_HEAD_DIM, MOBA_BLOCK, MOBA_Q_CHUNK
    pos = jnp.arange(S, dtype=jnp.float32)
    qkv = (h @ w_qkv).reshape(B, S, 3, H, Dh).transpose(2, 0, 3, 1, 4)
    q = rope(qkv[0], pos)
    k = rope(qkv[1], pos)
    v = qkv[2]
    nb = -(-S // L)
    pad = nb * L - S
    k_p = jnp.pad(k, ((0, 0), (0, 0), (0, pad), (0, 0)))
    v_p = jnp.pad(v, ((0, 0), (0, 0), (0, pad), (0, 0)))
    k_blk = k_p.reshape(B, H, nb, L, Dh)
    v_blk = v_p.reshape(B, H, nb, L, Dh)
    k_mean = jnp.mean(k_blk.astype(jnp.float32), axis=3)
    k_top = min(MOBA_TOPK, nb)
    scale = Dh ** -0.5
    bi = jnp.arange(B)[:, None, None, None]
    hi = jnp.arange(H)[None, :, None, None]

    def chunk(c):
        start = c * QC
        qc = lax.dynamic_slice_in_dim(q, start, QC, axis=2)
        qpos = start + jnp.arange(QC)
        own = start // L
        gate = jnp.einsum('bhqd,bhnd->bhqn', qc.astype(jnp.float32), k_mean)
        fully_past = jnp.arange(nb)[None, None, None, :] < own
        gate = jnp.where(fully_past, gate, -jnp.inf)
        _, sel = lax.top_k(gate, k_top)
        sel_valid = sel < own
        k_sel = k_blk[bi, hi, sel]
        v_sel = v_blk[bi, hi, sel]
        s_sel = jnp.einsum('bhqd,bhqkld->bhqkl', qc, k_sel).astype(jnp.float32) * scale
        s_sel = jnp.where(sel_valid[..., None], s_sel, NEG_INF).reshape(B, H, QC, k_top * L)
        k_own = lax.dynamic_slice_in_dim(k_p, own * L, L, axis=2)
        v_own = lax.dynamic_slice_in_dim(v_p, own * L, L, axis=2)
        s_own = jnp.einsum('bhqd,bhld->bhql', qc, k_own).astype(jnp.float32) * scale
        kpos = own * L + jnp.arange(L)
        s_own = jnp.where(kpos[None, :] <= qpos[:, None], s_own, NEG_INF)
        pr = jax.nn.softmax(jnp.concatenate([s_sel, s_own], axis=-1), axis=-1)
        p_sel = pr[..., :k_top * L].reshape(B, H, QC, k_top, L).astype(v.dtype)
        p_own = pr[..., k_top * L:].astype(v.dtype)
        return (jnp.einsum('bhqkl,bhqkld->bhqd', p_sel, v_sel)
                + jnp.einsum('bhql,bhld->bhqd', p_own, v_own))

    o = lax.map(chunk, jnp.arange(S // QC))
    o = o.transpose(1, 0, 3, 2, 4).reshape(B, S, H * Dh)
    return o @ w_o


def mla_attention(h, w_dq, q_norm, w_uq, w_dkv, kv_norm, w_ukv, w_o):
    B, S, _ = h.shape
    H, QB = MLA_HEADS, MLA_Q_BLOCK
    pos = jnp.arange(S, dtype=jnp.float32)
    cq = rms_norm(h @ w_dq, q_norm)
    q = (cq @ w_uq).reshape(B, S, H, MLA_NOPE + MLA_ROPE).transpose(0, 2, 1, 3)
    q_nope = q[..., :MLA_NOPE]
    q_rope = rope(q[..., MLA_NOPE:], pos)
    kv_a = h @ w_dkv
    ckv = rms_norm(kv_a[..., :MLA_KV_RANK], kv_norm)
    k_rope = rope(kv_a[..., MLA_KV_RANK:], pos)
    kv = (ckv @ w_ukv).reshape(B, S, H, MLA_NOPE + MLA_V).transpose(0, 2, 1, 3)
    k_nope = kv[..., :MLA_NOPE]
    v = kv[..., MLA_NOPE:]
    scale = (MLA_NOPE + MLA_ROPE) ** -0.5
    kpos = jnp.arange(S)

    def qblock(c):
        start = c * QB
        qn = lax.dynamic_slice_in_dim(q_nope, start, QB, axis=2)
        qr = lax.dynamic_slice_in_dim(q_rope, start, QB, axis=2)
        s = (jnp.einsum('bhqd,bhkd->bhqk', qn, k_nope)
             + jnp.einsum('bhqd,bkd->bhqk', qr, k_rope)).astype(jnp.float32) * scale
        qpos = start + jnp.arange(QB)
        s = jnp.where(kpos[None, :] <= qpos[:, None], s, NEG_INF)
        pr = jax.nn.softmax(s, axis=-1).astype(v.dtype)
        return jnp.einsum('bhqk,bhkd->bhqd', pr, v)

    o = lax.map(qblock, jnp.arange(S // QB))
    o = o.transpose(1, 0, 3, 2, 4).reshape(B, S, H * MLA_V)
    return o @ w_o


def hier_moe(h, w_group, w_expert, w_gate, w_up, w_down):
    B, S, _ = h.shape
    hf = h.astype(jnp.float32)
    g_logits = hf @ w_group.astype(jnp.float32)
    g_prob = jax.nn.softmax(g_logits, axis=-1)
    g_sel = jnp.argmax(g_logits, axis=-1)
    g_w = jnp.take_along_axis(g_prob, g_sel[..., None], axis=-1)
    e_logits = (hf @ w_expert.astype(jnp.float32)).reshape(B, S, N_GROUPS, EXPERTS_PER_GROUP)
    e_in = jnp.take_along_axis(e_logits, g_sel[..., None, None], axis=2)[..., 0, :]
    top_v, top_i = lax.top_k(e_in, TOPK_IN_GROUP)
    top_w = jax.nn.softmax(top_v, axis=-1) * g_w
    expert_id = g_sel[..., None] * EXPERTS_PER_GROUP + top_i
    combine = jnp.sum(jax.nn.one_hot(expert_id, N_EXPERTS, dtype=jnp.float32) * top_w[..., None], axis=-2)
    a = jnp.einsum('bsd,edf->bsef', h, w_gate)
    u = jnp.einsum('bsd,edf->bsef', h, w_up)
    act = jax.nn.silu(a) * u * combine.astype(h.dtype)[..., None]
    return jnp.einsum('bsef,efd->bsd', act, w_down)


def setup_inputs(seed: int = 0) -> dict:
    key = jax.random.key(seed)
    ks = iter(jax.random.split(key, 32))
    f32 = jnp.float32

    def w(shape, fan_in):
        return jax.random.normal(next(ks), shape, f32) * (fan_in ** -0.5)

    def gain(shape):
        return 1.0 + 0.02 * jax.random.normal(next(ks), shape, f32)

    D = D_MODEL
    return {
        "x": jax.random.normal(next(ks), (BATCH, SEQ, D), f32),
        "p": jax.random.normal(next(ks), (DEPTH, BATCH, SEQ, PLE_DIM), f32),
        "attn_norm": gain((DEPTH, D)),
        "ffn_norm": gain((DEPTH, D)),
        "ple_norm": gain((DEPTH, D)),
        "final_norm": gain((D,)),
        "moba_wqkv": w((N_MOBA_LAYERS, D, 3 * MOBA_HEADS * MOBA_HEAD_DIM), D),
        "moba_wo": w((N_MOBA_LAYERS, MOBA_HEADS * MOBA_HEAD_DIM, D), MOBA_HEADS * MOBA_HEAD_DIM),
        "mla_wdq": w((N_MLA_LAYERS, D, MLA_Q_RANK), D),
        "mla_qnorm": gain((N_MLA_LAYERS, MLA_Q_RANK)),
        "mla_wuq": w((N_MLA_LAYERS, MLA_Q_RANK, MLA_HEADS * (MLA_NOPE + MLA_ROPE)), MLA_Q_RANK),
        "mla_wdkv": w((N_MLA_LAYERS, D, MLA_KV_RANK + MLA_ROPE), D),
        "mla_kvnorm": gain((N_MLA_LAYERS, MLA_KV_RANK)),
        "mla_wukv": w((N_MLA_LAYERS, MLA_KV_RANK, MLA_HEADS * (MLA_NOPE + MLA_V)), MLA_KV_RANK),
        "mla_wo": w((N_MLA_LAYERS, MLA_HEADS * MLA_V, D), MLA_HEADS * MLA_V),
        "moe_wgroup": w((DEPTH, D, N_GROUPS), D),
        "moe_wexpert": w((DEPTH, D, N_EXPERTS), D),
        "moe_wgate": w((DEPTH, N_EXPERTS, D, EXPERT_FF), D),
        "moe_wup": w((DEPTH, N_EXPERTS, D, EXPERT_FF), D),
        "moe_wdown": w((DEPTH, N_EXPERTS, EXPERT_FF, D), EXPERT_FF),
        "ple_gate": w((DEPTH, D, D), D),
        "ple_proj": w((DEPTH, PLE_DIM, D), PLE_DIM),
    }


def reference(x, p, attn_norm, ffn_norm, ple_norm, final_norm, moba_wqkv, moba_wo,
              mla_wdq, mla_qnorm, mla_wuq, mla_wdkv, mla_kvnorm, mla_wukv, mla_wo,
              moe_wgroup, moe_wexpert, moe_wgate, moe_wup, moe_wdown, ple_gate, ple_proj):
    h = x
    for i in range(DEPTH):
        hn = rms_norm(h, attn_norm[i])
        j = i // N_MIXERS
        if i % N_MIXERS == 0:
            mix = moba_attention(hn, moba_wqkv[j], moba_wo[j])
        else:
            mix = mla_attention(hn, mla_wdq[j], mla_qnorm[j], mla_wuq[j], mla_wdkv[j],
                                mla_kvnorm[j], mla_wukv[j], mla_wo[j])
        h = h + mix
        h = h + hier_moe(rms_norm(h, ffn_norm[i]), moe_wgroup[i], moe_wexpert[i],
                         moe_wgate[i], moe_wup[i], moe_wdown[i])
        gate = jax.nn.sigmoid(rms_norm(h, ple_norm[i]) @ ple_gate[i])
        h = h + gate * (p[i] @ ple_proj[i])
    return rms_norm(h, final_norm)
```

```python
import functools

import numpy as np
import jax
import jax.numpy as jnp
from jax import lax
from jax.experimental import pallas as pl
from jax.experimental.pallas import tpu as pltpu

F32 = jnp.float32
BF16 = jnp.bfloat16

D_MODEL = 1024
N_LAYERS = 2
EPS = 1e-6
THETA = 10000.0
NEG = -1e30

HEADS = 8
HEAD_DIM = 128
KV_BLOCK = 256
MOBA_TOPK = 3

MLA_Q_RANK = 384
MLA_KV_RANK = 256
MLA_NOPE = 128
MLA_ROPE = 64
MLA_QK_PAD = 256

N_GROUPS = 4
PER_GROUP = 4
N_EXPERTS = 16
EXPERT_FF = 256
ROUTER_LANES = 128

PLE_DIM = 256

PROJ_TILE = 512
ATTN_Q_TILE = 512
POST_TILE = 512

VMEM_LIMIT = 56 * 1024 * 1024


def _rms(x, g):
    return x * lax.rsqrt(jnp.mean(x * x, axis=-1, keepdims=True) + EPS) * g


def _nt_dot(a, b):
    return lax.dot_general(a, b, (((1,), (1,)), ((), ())), preferred_element_type=F32)


def _moba_proj_kernel(x_ref, g_ref, w_ref, cos_ref, sin_ref, q_ref, k_ref, vt_ref):
    hn = _rms(x_ref[...], g_ref[...])
    qkv = jnp.dot(hn.astype(BF16), w_ref[...], preferred_element_type=F32)
    cos = cos_ref[...]
    sin = sin_ref[...]
    scale = HEAD_DIM ** -0.5
    hd = HEADS * HEAD_DIM
    for h in range(HEADS):
        lo = h * HEAD_DIM
        qh = qkv[:, lo:lo + HEAD_DIM]
        qh = qh * cos + pltpu.roll(qh, HEAD_DIM // 2, 1) * sin
        q_ref[0, h] = (qh * scale).astype(BF16)
        kh = qkv[:, hd + lo:hd + lo + HEAD_DIM]
        kh = (kh * cos + pltpu.roll(kh, HEAD_DIM // 2, 1) * sin).astype(BF16)
        vh = qkv[:, 2 * hd + lo:2 * hd + lo + HEAD_DIM]
        for c in range(PROJ_TILE // KV_BLOCK):
            rows = slice(c * KV_BLOCK, (c + 1) * KV_BLOCK)
            k_ref[0, h, c] = kh[rows]
            vt_ref[0, h, c] = vh[rows].T.astype(BF16)


def _moba_proj(x, g, w, cos2, sin2, batch, seq):
    tiles = seq // PROJ_TILE
    cpt = PROJ_TILE // KV_BLOCK
    nblk = seq // KV_BLOCK
    return pl.pallas_call(
        _moba_proj_kernel,
        grid=(batch, tiles),
        in_specs=[
            pl.BlockSpec((PROJ_TILE, D_MODEL), lambda b, t: (b * tiles + t, 0)),
            pl.BlockSpec((1, D_MODEL), lambda b, t: (0, 0)),
            pl.BlockSpec((D_MODEL, 3 * HEADS * HEAD_DIM), lambda b, t: (0, 0)),
            pl.BlockSpec((PROJ_TILE, HEAD_DIM), lambda b, t: (t, 0)),
            pl.BlockSpec((PROJ_TILE, HEAD_DIM), lambda b, t: (t, 0)),
        ],
        out_specs=[
            pl.BlockSpec((1, HEADS, PROJ_TILE, HEAD_DIM), lambda b, t: (b, 0, t, 0)),
            pl.BlockSpec((1, HEADS, cpt, KV_BLOCK, HEAD_DIM), lambda b, t: (b, 0, t, 0, 0)),
            pl.BlockSpec((1, HEADS, cpt, HEAD_DIM, KV_BLOCK), lambda b, t: (b, 0, t, 0, 0)),
        ],
        out_shape=[
            jax.ShapeDtypeStruct((batch, HEADS, seq, HEAD_DIM), BF16),
            jax.ShapeDtypeStruct((batch, HEADS, nblk, KV_BLOCK, HEAD_DIM), BF16),
            jax.ShapeDtypeStruct((batch, HEADS, nblk, HEAD_DIM, KV_BLOCK), BF16),
        ],
        compiler_params=pltpu.CompilerParams(
            dimension_semantics=("arbitrary", "arbitrary"), vmem_limit_bytes=VMEM_LIMIT),
        name="moba_proj",
    )(x, g, w, cos2, sin2)


def _attn_step(q, kb, vtb, carry, bias_row=None, causal=None):
    m, l, acc = carry
    s = _nt_dot(kb, q)
    if bias_row is not None:
        s = s + bias_row
    if causal is not None:
        s = jnp.where(causal, s, NEG)
    m_new = jnp.maximum(m, jnp.max(s, axis=0, keepdims=True))
    alpha = jnp.exp(m - m_new)
    p = jnp.exp(s - m_new)
    l = alpha * l + jnp.sum(p, axis=0, keepdims=True)
    acc = alpha * acc + jnp.dot(vtb, p.astype(BF16), preferred_element_type=F32)
    return m_new, l, acc


def _attn_init(tq, dv):
    return (jnp.full((1, tq), NEG, F32), jnp.zeros((1, tq), F32), jnp.zeros((dv, tq), F32))


def _causal_mask(kv_block_idx, q_tile_idx, tq):
    kpos = kv_block_idx * KV_BLOCK + lax.broadcasted_iota(jnp.int32, (KV_BLOCK, tq), 0)
    qpos = q_tile_idx * tq + lax.broadcasted_iota(jnp.int32, (KV_BLOCK, tq), 1)
    return kpos <= qpos


def _moba_attn_kernel(q_ref, k_ref, vt_ref, o_ref, kmean_ref, bias_ref):
    i = pl.program_id(2)
    tq = ATTN_Q_TILE
    nblk = k_ref.shape[2]
    bpt = tq // KV_BLOCK

    @pl.when(i == 0)
    def _():
        for n in range(nblk):
            kmean_ref[n:n + 1, :] = jnp.mean(k_ref[0, 0, n].astype(F32), axis=0, keepdims=True)

    q = q_ref[0, 0]
    km = kmean_ref[...]
    km_hi = km.astype(BF16)
    km_lo = (km - km_hi.astype(F32)).astype(BF16)
    gate = _nt_dot(km_hi, q) + _nt_dot(km_lo, q)
    nidx = lax.broadcasted_iota(jnp.int32, (nblk, tq), 0)
    own = (i * tq + lax.broadcasted_iota(jnp.int32, (nblk, tq), 1)) >> 8
    past = nidx < own
    g1 = jnp.where(past, gate, -jnp.inf)
    thr = jnp.max(g1, axis=0, keepdims=True)
    g = g1
    for _ in range(MOBA_TOPK - 1):
        g = jnp.where(g >= thr, -jnp.inf, g)
        thr = jnp.max(g, axis=0, keepdims=True)
    keep = (past & (g1 >= thr)) | (nidx == own)
    bias_ref[...] = jnp.where(keep, 0.0, NEG)

    def body(j, carry):
        return _attn_step(q, k_ref[0, 0, j], vt_ref[0, 0, j], carry,
                          bias_row=bias_ref[pl.ds(j, 1), :])

    carry = lax.fori_loop(0, i * bpt, body, _attn_init(tq, HEAD_DIM))
    for d in range(bpt):
        j = i * bpt + d
        carry = _attn_step(q, k_ref[0, 0, j], vt_ref[0, 0, j], carry,
                           bias_row=bias_ref[pl.ds(j, 1), :], causal=_causal_mask(j, i, tq))
    _, l, acc = carry
    o_ref[0] = (acc / l).T.astype(BF16)


def _moba_attn(q, k, vt):
    batch, heads, seq, dh = q.shape
    nblk = seq // KV_BLOCK
    tq = ATTN_Q_TILE
    return pl.pallas_call(
        _moba_attn_kernel,
        grid=(batch, heads, seq // tq),
        in_specs=[
            pl.BlockSpec((1, 1, tq, dh), lambda b, h, i: (b, h, i, 0)),
            pl.BlockSpec((1, 1, nblk, KV_BLOCK, dh), lambda b, h, i: (b, h, 0, 0, 0)),
            pl.BlockSpec((1, 1, nblk, dh, KV_BLOCK), lambda b, h, i: (b, h, 0, 0, 0)),
        ],
        out_specs=pl.BlockSpec((1, tq, dh), lambda b, h, i: (b, i, h)),
        out_shape=jax.ShapeDtypeStruct((batch, seq, heads * dh), BF16),
        scratch_shapes=[pltpu.VMEM((nblk, dh), F32), pltpu.VMEM((nblk, tq), F32)],
        compiler_params=pltpu.CompilerParams(
            dimension_semantics=("arbitrary", "arbitrary", "arbitrary"),
            vmem_limit_bytes=VMEM_LIMIT),
        name="moba_attn",
    )(q, k, vt)


def _rope_pairs32(x, cos, sin_lo, sin_hi):
    n = x.shape[1]
    return x * cos + pltpu.roll(x, n - 32, 1) * sin_lo + pltpu.roll(x, 32, 1) * sin_hi


def _mla_proj_kernel(x_ref, g_ref, wa_ref, qn_ref, kvn_ref, wuq_ref, wukv_ref,
                     cos_ref, slo_ref, shi_ref, q_ref, k_ref, vt_ref):
    hn = _rms(x_ref[...], g_ref[...])
    r = jnp.dot(hn.astype(BF16), wa_ref[...], preferred_element_type=F32)
    cq = _rms(r[:, :MLA_Q_RANK], qn_ref[...])
    ckv = _rms(r[:, MLA_Q_RANK:MLA_Q_RANK + MLA_KV_RANK], kvn_ref[...])
    kr = r[:, MLA_Q_RANK + MLA_KV_RANK:]
    q = jnp.dot(cq.astype(BF16), wuq_ref[...], preferred_element_type=F32)
    kv = jnp.dot(ckv.astype(BF16), wukv_ref[...], preferred_element_type=F32)
    cos = cos_ref[...]
    slo = slo_ref[...]
    shi = shi_ref[...]
    nn = HEADS * MLA_NOPE
    reps = HEADS * MLA_ROPE // 128
    qr = _rope_pairs32(q[:, nn:], jnp.concatenate([cos] * reps, axis=1),
                       jnp.concatenate([slo] * reps, axis=1), jnp.concatenate([shi] * reps, axis=1))
    kr = _rope_pairs32(kr, cos, slo, shi)
    scale = (MLA_NOPE + MLA_ROPE) ** -0.5
    low_half = lax.broadcasted_iota(jnp.int32, kr.shape, 1) < MLA_ROPE
    for h in range(HEADS):
        pair = qr[:, (h // 2) * 128:(h // 2 + 1) * 128]
        if h % 2:
            pair = pltpu.roll(pair, MLA_ROPE, 1)
        qh = jnp.concatenate(
            [q[:, h * MLA_NOPE:(h + 1) * MLA_NOPE], jnp.where(low_half, pair, 0.0)], axis=1)
        q_ref[0, h] = (qh * scale).astype(BF16)
        kh = jnp.concatenate([kv[:, h * MLA_NOPE:(h + 1) * MLA_NOPE], kr], axis=1).astype(BF16)
        vh = kv[:, nn + h * HEAD_DIM:nn + (h + 1) * HEAD_DIM]
        for c in range(PROJ_TILE // KV_BLOCK):
            rows = slice(c * KV_BLOCK, (c + 1) * KV_BLOCK)
            k_ref[0, h, c] = kh[rows]
            vt_ref[0, h, c] = vh[rows].T.astype(BF16)


def _mla_proj(x, g, wa, qn, kvn, wuq, wukv, cos, slo, shi, batch, seq):
    tiles = seq // PROJ_TILE
    cpt = PROJ_TILE // KV_BLOCK
    nblk = seq // KV_BLOCK
    full = lambda a: pl.BlockSpec(a.shape, lambda b, t: (0,) * a.ndim)
    tab = pl.BlockSpec((PROJ_TILE, 128), lambda b, t: (t, 0))
    return pl.pallas_call(
        _mla_proj_kernel,
        grid=(batch, tiles),
        in_specs=[
            pl.BlockSpec((PROJ_TILE, D_MODEL), lambda b, t: (b * tiles + t, 0)),
            full(g), full(wa), full(qn), full(kvn), full(wuq), full(wukv), tab, tab, tab,
        ],
        out_specs=[
            pl.BlockSpec((1, HEADS, PROJ_TILE, MLA_QK_PAD), lambda b, t: (b, 0, t, 0)),
            pl.BlockSpec((1, HEADS, cpt, KV_BLOCK, MLA_QK_PAD), lambda b, t: (b, 0, t, 0, 0)),
            pl.BlockSpec((1, HEADS, cpt, HEAD_DIM, KV_BLOCK), lambda b, t: (b, 0, t, 0, 0)),
        ],
        out_shape=[
            jax.ShapeDtypeStruct((batch, HEADS, seq, MLA_QK_PAD), BF16),
            jax.ShapeDtypeStruct((batch, HEADS, nblk, KV_BLOCK, MLA_QK_PAD), BF16),
            jax.ShapeDtypeStruct((batch, HEADS, nblk, HEAD_DIM, KV_BLOCK), BF16),
        ],
        compiler_params=pltpu.CompilerParams(
            dimension_semantics=("arbitrary", "arbitrary"), vmem_limit_bytes=VMEM_LIMIT),
        name="mla_proj",
    )(x, g, wa, qn, kvn, wuq, wukv, cos, slo, shi)


def _causal_attn_kernel(q_ref, k_ref, vt_ref, o_ref):
    i = pl.program_id(2)
    tq = ATTN_Q_TILE
    bpt = tq // KV_BLOCK
    q = q_ref[0, 0]

    def body(j, carry):
        return _attn_step(q, k_ref[0, 0, j], vt_ref[0, 0, j], carry)

    carry = lax.fori_loop(0, i * bpt, body, _attn_init(tq, HEAD_DIM))
    for d in range(bpt):
        j = i * bpt + d
        carry = _attn_step(q, k_ref[0, 0, j], vt_ref[0, 0, j], carry, causal=_causal_mask(j, i, tq))
    _, l, acc = carry
    o_ref[0] = (acc / l).T.astype(BF16)


def _causal_attn(q, k, vt):
    batch, heads, seq, dqk = q.shape
    nblk = seq // KV_BLOCK
    dv = vt.shape[3]
    tq = ATTN_Q_TILE
    return pl.pallas_call(
        _causal_attn_kernel,
        grid=(batch, heads, seq // tq),
        in_specs=[
            pl.BlockSpec((1, 1, tq, dqk), lambda b, h, i: (b, h, i, 0)),
            pl.BlockSpec((1, 1, nblk, KV_BLOCK, dqk), lambda b, h, i: (b, h, 0, 0, 0)),
            pl.BlockSpec((1, 1, nblk, dv, KV_BLOCK), lambda b, h, i: (b, h, 0, 0, 0)),
        ],
        out_specs=pl.BlockSpec((1, tq, dv), lambda b, h, i: (b, i, h)),
        out_shape=jax.ShapeDtypeStruct((batch, seq, heads * dv), BF16),
        compiler_params=pltpu.CompilerParams(
            dimension_semantics=("arbitrary", "arbitrary", "arbitrary"),
            vmem_limit_bytes=VMEM_LIMIT),
        name="mla_attn",
    )(q, k, vt)


def _post_attn_kernel(h_ref, o_ref, wo_ref, g_ref, wr_hi_ref, wr_lo_ref, h1_ref, hn_ref, comb_ref):
    h1 = h_ref[...] + jnp.dot(o_ref[...], wo_ref[...], preferred_element_type=F32)
    h1_ref[...] = h1
    hn = _rms(h1, g_ref[...])
    hi = hn.astype(BF16)
    hn_ref[...] = hi
    lo = (hn - hi.astype(F32)).astype(BF16)
    w_hi = wr_hi_ref[...]
    logits = (jnp.dot(hi, w_hi, preferred_element_type=F32)
              + jnp.dot(hi, wr_lo_ref[...], preferred_element_type=F32)
              + jnp.dot(lo, w_hi, preferred_element_type=F32))
    lane_i = lax.broadcasted_iota(jnp.int32, logits.shape, 1)
    lane = lane_i.astype(F32)
    is_group = (lane_i >= N_EXPERTS) & (lane_i < N_EXPERTS + N_GROUPS)

    def first_max(mask):
        v = jnp.max(jnp.where(mask, logits, -jnp.inf), axis=1, keepdims=True)
        idx = jnp.min(jnp.where(mask & (logits == v), lane, 1e9), axis=1, keepdims=True)
        return v, idx

    g_max, g_lane = first_max(is_group)
    g_sum = jnp.sum(jnp.where(is_group, jnp.exp(logits - g_max), 0.0), axis=1, keepdims=True)
    g_w = 1.0 / g_sum
    group_of_lane = (lane_i >> 2).astype(F32)
    in_group = (lane_i < N_EXPERTS) & (group_of_lane == (g_lane - N_EXPERTS))
    v1, i1 = first_max(in_group)
    v2, i2 = first_max(in_group & (lane != i1))
    t = jnp.exp(v2 - v1)
    w1 = g_w / (1.0 + t)
    w2 = w1 * t
    comb_ref[...] = jnp.where(lane == i1, w1, 0.0) + jnp.where(lane == i2, w2, 0.0)


def _post_attn(h, o, wo, g, wr_hi, wr_lo):
    n = h.shape[0]
    tile = POST_TILE
    row = lambda w: pl.BlockSpec((tile, w), lambda t: (t, 0))
    full = lambda a: pl.BlockSpec(a.shape, lambda t: (0,) * a.ndim)
    return pl.pallas_call(
        _post_attn_kernel,
        grid=(n // tile,),
        in_specs=[row(D_MODEL), row(D_MODEL), full(wo), full(g), full(wr_hi), full(wr_lo)],
        out_specs=[row(D_MODEL), row(D_MODEL), row(ROUTER_LANES)],
        out_shape=[
            jax.ShapeDtypeStruct((n, D_MODEL), F32),
            jax.ShapeDtypeStruct((n, D_MODEL), BF16),
            jax.ShapeDtypeStruct((n, ROUTER_LANES), F32),
        ],
        compiler_params=pltpu.CompilerParams(
            dimension_semantics=("arbitrary",), vmem_limit_bytes=VMEM_LIMIT),
        name="post_attn",
    )(h, o, wo, g, wr_hi, wr_lo)


def _moe_ple_kernel(hn_ref, comb_ref, h1_ref, p_ref, wg_ref, wu_ref, wd_ref, pn_ref, pg_ref, pp_ref,
                    fn_ref, out_ref, acc_ref, *, final):
    grp = pl.program_id(1)
    x = hn_ref[...]
    a = jnp.dot(x, wg_ref[0], preferred_element_type=F32)
    u = jnp.dot(x, wu_ref[0], preferred_element_type=F32)
    act = a * jax.nn.sigmoid(a) * u
    comb = comb_ref[...]
    lane = lax.broadcasted_iota(jnp.int32, comb.shape, 1)
    parts = []
    for j in range(PER_GROUP):
        c = jnp.sum(jnp.where(lane == grp * PER_GROUP + j, comb, 0.0), axis=1, keepdims=True)
        parts.append((act[:, j * EXPERT_FF:(j + 1) * EXPERT_FF] * c).astype(BF16))
    y = jnp.dot(jnp.concatenate(parts, axis=1), wd_ref[0], preferred_element_type=F32)

    @pl.when(grp == 0)
    def _():
        acc_ref[...] = y

    @pl.when(grp > 0)
    def _():
        acc_ref[...] += y

    @pl.when(grp == N_GROUPS - 1)
    def _():
        h2 = h1_ref[...] + acc_ref[...]
        gate = jax.nn.sigmoid(
            jnp.dot(_rms(h2, pn_ref[...]).astype(BF16), pg_ref[...], preferred_element_type=F32))
        proj = jnp.dot(p_ref[...].astype(BF16), pp_ref[...], preferred_element_type=F32)
        h3 = h2 + gate * proj
        out_ref[...] = _rms(h3, fn_ref[...]) if final else h3


def _moe_ple(hn, comb, h1, p, wg, wu, wd, pn, pg, pp, fn, final):
    n = hn.shape[0]
    tile = POST_TILE
    row = lambda w: pl.BlockSpec((tile, w), lambda t, e: (t, 0))
    full = lambda a: pl.BlockSpec(a.shape, lambda t, e: (0,) * a.ndim)
    grp = lambda a: pl.BlockSpec((1,) + a.shape[1:], lambda t, e: (e, 0, 0))
    return pl.pallas_call(
        functools.partial(_moe_ple_kernel, final=final),
        grid=(n // tile, N_GROUPS),
        in_specs=[row(D_MODEL), row(ROUTER_LANES), row(D_MODEL), row(PLE_DIM),
                  grp(wg), grp(wu), grp(wd), full(pn), full(pg), full(pp), full(fn)],
        out_specs=row(D_MODEL),
        out_shape=jax.ShapeDtypeStruct((n, D_MODEL), F32),
        scratch_shapes=[pltpu.VMEM((tile, D_MODEL), F32)],
        compiler_params=pltpu.CompilerParams(
            dimension_semantics=("arbitrary", "arbitrary"), vmem_limit_bytes=VMEM_LIMIT),
        name="moe_ple",
    )(hn, comb, h1, p, wg, wu, wd, pn, pg, pp, fn)


def _rope_tables(seq, half):
    inv = THETA ** (-jnp.arange(half, dtype=F32) / half)
    ang = jnp.arange(seq, dtype=F32)[:, None] * inv[None, :]
    return jnp.cos(ang), jnp.sin(ang)


def _split_bf16(w):
    hi = w.astype(BF16)
    return hi, (w - hi.astype(F32)).astype(BF16)


def kernel(x, p, attn_norm, ffn_norm, ple_norm, final_norm, moba_wqkv, moba_wo, mla_wdq, mla_qnorm,
           mla_wuq, mla_wdkv, mla_kvnorm, mla_wukv, mla_wo, moe_wgroup, moe_wexpert, moe_wgate,
           moe_wup, moe_wdown, ple_gate, ple_proj):
    batch, seq, d = x.shape
    n = batch * seq
    h = x.reshape(n, d)

    cos64, sin64 = _rope_tables(seq, HEAD_DIM // 2)
    moba_cos = jnp.concatenate([cos64, cos64], axis=1)
    moba_sin = jnp.concatenate([-sin64, sin64], axis=1)
    cos32, sin32 = _rope_tables(seq, MLA_ROPE // 2)
    z32 = jnp.zeros_like(sin32)
    mla_cos = jnp.concatenate([cos32] * 4, axis=1)
    mla_slo = jnp.concatenate([-sin32, z32, -sin32, z32], axis=1)
    mla_shi = jnp.concatenate([z32, sin32, z32, sin32], axis=1)

    row = lambda v: v.reshape(1, -1)

    for layer in range(N_LAYERS):
        j = layer // 2
        if layer % 2 == 0:
            q, k, vt = _moba_proj(h, row(attn_norm[layer]), moba_wqkv[j].astype(BF16),
                                  moba_cos, moba_sin, batch, seq)
            o = _moba_attn(q, k, vt)
            wo = moba_wo[j]
        else:
            wuq = mla_wuq[j].reshape(MLA_Q_RANK, HEADS, MLA_NOPE + MLA_ROPE)
            wuq = jnp.concatenate([wuq[:, :, :MLA_NOPE].reshape(MLA_Q_RANK, -1),
                                   wuq[:, :, MLA_NOPE:].reshape(MLA_Q_RANK, -1)], axis=1)
            wukv = mla_wukv[j].reshape(MLA_KV_RANK, HEADS, MLA_NOPE + HEAD_DIM)
            wukv = jnp.concatenate([wukv[:, :, :MLA_NOPE].reshape(MLA_KV_RANK, -1),
                                    wukv[:, :, MLA_NOPE:].reshape(MLA_KV_RANK, -1)], axis=1)
            wa = jnp.concatenate(
                [mla_wdq[j], mla_wdkv[j], jnp.zeros((d, 128 - MLA_ROPE), F32)], axis=1)
            q, k, vt = _mla_proj(h, row(attn_norm[layer]), wa.astype(BF16), row(mla_qnorm[j]),
                                 row(mla_kvnorm[j]), wuq.astype(BF16), wukv.astype(BF16),
                                 mla_cos, mla_slo, mla_shi, batch, seq)
            o = _causal_attn(q, k, vt)
            wo = mla_wo[j]

        wr = jnp.concatenate(
            [moe_wexpert[layer], moe_wgroup[layer],
             jnp.zeros((d, ROUTER_LANES - N_EXPERTS - N_GROUPS), F32)], axis=1)
        wr_hi, wr_lo = _split_bf16(wr)
        h1, hn, comb = _post_attn(h, o.reshape(n, -1), wo.astype(BF16), row(ffn_norm[layer]), wr_hi, wr_lo)

        def by_group(w):
            w = w.reshape(N_GROUPS, PER_GROUP, d, EXPERT_FF).transpose(0, 2, 1, 3)
            return w.reshape(N_GROUPS, d, PER_GROUP * EXPERT_FF).astype(BF16)

        wd = moe_wdown[layer].reshape(N_GROUPS, PER_GROUP * EXPERT_FF, d).astype(BF16)
        h = _moe_ple(hn, comb, h1, p[layer].reshape(n, PLE_DIM), by_group(moe_wgate[layer]),
                     by_group(moe_wup[layer]), wd, row(ple_norm[layer]), ple_gate[layer].astype(BF16),
                     ple_proj[layer].astype(BF16), row(final_norm), layer == N_LAYERS - 1)

    return h.reshape(batch, seq, d)
```

```python
import functools

import jax
import jax.numpy as jnp
from jax import lax
from jax.experimental import pallas as pl
from jax.experimental.pallas import tpu as pltpu

F32 = jnp.float32
BF16 = jnp.bfloat16

D_MODEL = 1024
N_LAYERS = 2
EPS = 1e-6
THETA = 10000.0
NEG = -1e30
LOG2E = 1.4426950408889634

HEADS = 8
HEAD_DIM = 128
MOBA_BLOCK = 256
MOBA_TOPK = 3

MLA_Q_RANK = 384
MLA_KV_RANK = 256
MLA_NOPE = 128
MLA_ROPE = 64
MLA_QK_PAD = 256

N_GROUPS = 4
PER_GROUP = 4
N_EXPERTS = 16
EXPERT_FF = 256
ROUTER_LANES = 128

PLE_DIM = 256

TILE = 512
V_ROWS = HEAD_DIM + 16
ATTN_HEADS = 2
POST_TILE = 512

VMEM_LIMIT = 56 * 1024 * 1024


def _rms(x, g):
    return x * lax.rsqrt(jnp.mean(x * x, axis=-1, keepdims=True) + EPS) * g


def _nt_dot(a, b):
    return lax.dot_general(a, b, (((1,), (1,)), ((), ())), preferred_element_type=F32)


def _moba_proj_kernel(x_ref, g_ref, w_ref, cos_ref, sin_ref, qt_ref, k_ref, vt_ref):
    hn = _rms(x_ref[...], g_ref[...])
    qkv = jnp.dot(hn.astype(BF16), w_ref[...], preferred_element_type=F32)
    cos = cos_ref[...]
    sin = sin_ref[...]
    scale = HEAD_DIM ** -0.5 * LOG2E
    hd = HEADS * HEAD_DIM
    for h in range(HEADS):
        lo = h * HEAD_DIM
        qh = qkv[:, lo:lo + HEAD_DIM]
        qh = qh * cos + pltpu.roll(qh, HEAD_DIM // 2, 1) * sin
        qt_ref[0, h] = (qh * scale).T.astype(BF16)
        kh = qkv[:, hd + lo:hd + lo + HEAD_DIM]
        kh = (kh * cos + pltpu.roll(kh, HEAD_DIM // 2, 1) * sin).astype(BF16)
        vh = qkv[:, 2 * hd + lo:2 * hd + lo + HEAD_DIM]
        _store_kv(k_ref, vt_ref, h, kh, vh)


def _store_kv(k_ref, vt_ref, h, kh, vh):
    k_ref[0, h, 0] = kh
    vt_ref[0, h, 0] = jnp.concatenate(
        [vh.T, jnp.ones((V_ROWS - HEAD_DIM, TILE), F32)], axis=0).astype(BF16)


def _qkv_out(batch, seq, dqk):
    steps = seq // TILE
    specs = [
        pl.BlockSpec((1, HEADS, dqk, TILE), lambda b, t: (b, 0, 0, t)),
        pl.BlockSpec((1, HEADS, 1, TILE, dqk), lambda b, t: (b, 0, t, 0, 0)),
        pl.BlockSpec((1, HEADS, 1, V_ROWS, TILE), lambda b, t: (b, 0, t, 0, 0)),
    ]
    shapes = [
        jax.ShapeDtypeStruct((batch, HEADS, dqk, seq), BF16),
        jax.ShapeDtypeStruct((batch, HEADS, steps, TILE, dqk), BF16),
        jax.ShapeDtypeStruct((batch, HEADS, steps, V_ROWS, TILE), BF16),
    ]
    return specs, shapes


def _moba_proj(x, g, w, cos2, sin2, batch, seq):
    tiles = seq // TILE
    out_specs, out_shape = _qkv_out(batch, seq, HEAD_DIM)
    return pl.pallas_call(
        _moba_proj_kernel,
        grid=(batch, tiles),
        in_specs=[
            pl.BlockSpec((TILE, D_MODEL), lambda b, t: (b * tiles + t, 0)),
            pl.BlockSpec((1, D_MODEL), lambda b, t: (0, 0)),
            pl.BlockSpec((D_MODEL, 3 * HEADS * HEAD_DIM), lambda b, t: (0, 0)),
            pl.BlockSpec((TILE, HEAD_DIM), lambda b, t: (t, 0)),
            pl.BlockSpec((TILE, HEAD_DIM), lambda b, t: (t, 0)),
        ],
        out_specs=out_specs,
        out_shape=out_shape,
        compiler_params=pltpu.CompilerParams(
            dimension_semantics=("arbitrary", "arbitrary"), vmem_limit_bytes=VMEM_LIMIT),
        name="moba_proj",
    )(x, g, w, cos2, sin2)


def _attn_scratch(nh):
    return ([pltpu.VMEM((TILE, TILE), F32)] * (2 * nh) + [pltpu.VMEM((1, TILE), F32)] * (2 * nh)
            + [pltpu.VMEM((1, TILE), F32)] * nh + [pltpu.VMEM((V_ROWS, TILE), F32)] * nh)


def _attention(qt_ref, k_ref, vt_ref, o_ref, scratch, i, bias_ref=None):
    nh, _, tq = qt_ref.shape[1:]
    s_refs = (scratch[:nh], scratch[nh:2 * nh])
    smax_refs = (scratch[2 * nh:3 * nh], scratch[3 * nh:4 * nh])
    m_refs = scratch[4 * nh:5 * nh]
    acc_refs = scratch[5 * nh:6 * nh]

    def produce(h, step, par):
        s = jnp.dot(k_ref[0, h, step], qt_ref[0, h], preferred_element_type=F32)
        if bias_ref is not None:
            nb = TILE // MOBA_BLOCK
            b = bias_ref[h, pl.ds(step * nb, nb)]
            s = (s.reshape(nb, MOBA_BLOCK, tq) + b).reshape(TILE, tq)
        s_refs[par][h][...] = s
        smax_refs[par][h][...] = jnp.max(s, axis=0, keepdims=True)

    def consume(h, step, par, diagonal):
        s = s_refs[par][h][...]
        if diagonal:
            kpos = lax.broadcasted_iota(jnp.int32, s.shape, 0)
            qpos = lax.broadcasted_iota(jnp.int32, s.shape, 1)
            s = jnp.where(kpos <= qpos, s, NEG)
            s_max = jnp.max(s, axis=0, keepdims=True)
        else:
            s_max = smax_refs[par][h][...]
        m = m_refs[h][...]
        m_new = jnp.maximum(m, s_max)
        m_refs[h][...] = m_new
        p = jnp.exp2(s - m_new).astype(BF16)
        pv = jnp.dot(vt_ref[0, h, step], p, preferred_element_type=F32)
        acc_refs[h][...] = jnp.exp2(m - m_new) * acc_refs[h][...] + pv

    for h in range(nh):
        m_refs[h][...] = jnp.full((1, tq), NEG, F32)
        acc_refs[h][...] = jnp.zeros((V_ROWS, tq), F32)
        produce(h, 0, 0)

    def body(t, _):
        for par in range(2):
            @pl.when((t & 1) == par)
            def _():
                for h in range(nh):
                    produce(h, t + 1, 1 - par)
                    consume(h, t, par, False)
        return 0

    lax.fori_loop(0, i, body, 0)
    for par in range(2):
        @pl.when((i & 1) == par)
        def _():
            for h in range(nh):
                consume(h, i, par, True)

    for h in range(nh):
        acc = acc_refs[h][...]
        out = acc[:HEAD_DIM] / acc[HEAD_DIM:HEAD_DIM + 1]
        o_ref[0, :, h * HEAD_DIM:(h + 1) * HEAD_DIM] = out.T.astype(BF16)


def _moba_attn_kernel(qt_ref, k_ref, vt_ref, o_ref, kmean_ref, bias_ref, *scratch):
    i = pl.program_id(2)
    nh, _, tq = qt_ref.shape[1:]
    steps = k_ref.shape[2]
    nb = TILE // MOBA_BLOCK
    nblk = steps * nb

    @pl.when(i == 0)
    def _():
        for h in range(nh):
            for n in range(nblk):
                kb = k_ref[0, h, n // nb, (n % nb) * MOBA_BLOCK:(n % nb + 1) * MOBA_BLOCK, :]
                kmean_ref[h, n:n + 1, :] = jnp.mean(kb.astype(F32), axis=0, keepdims=True)

    nidx = lax.broadcasted_iota(jnp.int32, (nblk, tq), 0)
    own = (i * tq + lax.broadcasted_iota(jnp.int32, (nblk, tq), 1)) >> 8
    past = nidx < own
    for h in range(nh):
        qt = qt_ref[0, h]
        km = kmean_ref[h]
        km_hi = km.astype(BF16)
        km_lo = (km - km_hi.astype(F32)).astype(BF16)
        gate = (jnp.dot(km_hi, qt, preferred_element_type=F32)
                + jnp.dot(km_lo, qt, preferred_element_type=F32))
        g1 = jnp.where(past, gate, -jnp.inf)
        thr = jnp.max(g1, axis=0, keepdims=True)
        g = g1
        for _ in range(MOBA_TOPK - 1):
            g = jnp.where(g >= thr, -jnp.inf, g)
            thr = jnp.max(g, axis=0, keepdims=True)
        keep = (past & (g1 >= thr)) | (nidx == own)
        bias = jnp.where(keep, 0.0, NEG)
        for n in range(nblk):
            bias_ref[h, n] = bias[n:n + 1, :]

    _attention(qt_ref, k_ref, vt_ref, o_ref, scratch, i, bias_ref)


def _causal_attn_kernel(qt_ref, k_ref, vt_ref, o_ref, *scratch):
    _attention(qt_ref, k_ref, vt_ref, o_ref, scratch, pl.program_id(2))


def _attn_call(kernel_fn, name, qt, k, vt, scratch):
    batch, heads, dqk, seq = qt.shape
    steps = seq // TILE
    nh = ATTN_HEADS
    return pl.pallas_call(
        kernel_fn,
        grid=(batch, heads // nh, steps),
        in_specs=[
            pl.BlockSpec((1, nh, dqk, TILE), lambda b, h, i: (b, h, 0, i)),
            pl.BlockSpec((1, nh, steps, TILE, dqk), lambda b, h, i: (b, h, 0, 0, 0)),
            pl.BlockSpec((1, nh, steps, V_ROWS, TILE), lambda b, h, i: (b, h, 0, 0, 0)),
        ],
        out_specs=pl.BlockSpec((1, TILE, nh * HEAD_DIM), lambda b, h, i: (b, i, h)),
        out_shape=jax.ShapeDtypeStruct((batch, seq, heads * HEAD_DIM), BF16),
        scratch_shapes=scratch + _attn_scratch(nh),
        compiler_params=pltpu.CompilerParams(
            dimension_semantics=("arbitrary", "arbitrary", "arbitrary"),
            vmem_limit_bytes=VMEM_LIMIT),
        name=name,
    )(qt, k, vt)


def _moba_attn(qt, k, vt):
    seq = qt.shape[3]
    scratch = [pltpu.VMEM((ATTN_HEADS, seq // MOBA_BLOCK, HEAD_DIM), F32),
               pltpu.VMEM((ATTN_HEADS, seq // MOBA_BLOCK, 1, TILE), F32)]
    return _attn_call(_moba_attn_kernel, "moba_attn", qt, k, vt, scratch)


def _causal_attn(qt, k, vt):
    return _attn_call(_causal_attn_kernel, "mla_attn", qt, k, vt, [])


def _rope_pairs32(x, cos, sin_lo, sin_hi):
    n = x.shape[1]
    return x * cos + pltpu.roll(x, n - 32, 1) * sin_lo + pltpu.roll(x, 32, 1) * sin_hi


def _mla_proj_kernel(x_ref, g_ref, wa_ref, qn_ref, kvn_ref, wuq_ref, wukv_ref,
                     cos_ref, slo_ref, shi_ref, qt_ref, k_ref, vt_ref):
    hn = _rms(x_ref[...], g_ref[...])
    r = jnp.dot(hn.astype(BF16), wa_ref[...], preferred_element_type=F32)
    cq = _rms(r[:, :MLA_Q_RANK], qn_ref[...])
    ckv = _rms(r[:, MLA_Q_RANK:MLA_Q_RANK + MLA_KV_RANK], kvn_ref[...])
    kr = r[:, MLA_Q_RANK + MLA_KV_RANK:]
    q = jnp.dot(cq.astype(BF16), wuq_ref[...], preferred_element_type=F32)
    kv = jnp.dot(ckv.astype(BF16), wukv_ref[...], preferred_element_type=F32)
    cos = cos_ref[...]
    slo = slo_ref[...]
    shi = shi_ref[...]
    nn = HEADS * MLA_NOPE
    reps = HEADS * MLA_ROPE // 128
    qr = _rope_pairs32(q[:, nn:], jnp.concatenate([cos] * reps, axis=1),
                       jnp.concatenate([slo] * reps, axis=1), jnp.concatenate([shi] * reps, axis=1))
    kr = _rope_pairs32(kr, cos, slo, shi)
    scale = (MLA_NOPE + MLA_ROPE) ** -0.5 * LOG2E
    low_half = lax.broadcasted_iota(jnp.int32, kr.shape, 1) < MLA_ROPE
    for h in range(HEADS):
        pair = qr[:, (h // 2) * 128:(h // 2 + 1) * 128]
        if h % 2:
            pair = pltpu.roll(pair, MLA_ROPE, 1)
        qh = jnp.concatenate(
            [q[:, h * MLA_NOPE:(h + 1) * MLA_NOPE], jnp.where(low_half, pair, 0.0)], axis=1)
        qt_ref[0, h] = (qh * scale).T.astype(BF16)
        kh = jnp.concatenate([kv[:, h * MLA_NOPE:(h + 1) * MLA_NOPE], kr], axis=1).astype(BF16)
        vh = kv[:, nn + h * HEAD_DIM:nn + (h + 1) * HEAD_DIM]
        _store_kv(k_ref, vt_ref, h, kh, vh)


def _mla_proj(x, g, wa, qn, kvn, wuq, wukv, cos, slo, shi, batch, seq):
    tiles = seq // TILE
    full = lambda a: pl.BlockSpec(a.shape, lambda b, t: (0,) * a.ndim)
    tab = pl.BlockSpec((TILE, 128), lambda b, t: (t, 0))
    out_specs, out_shape = _qkv_out(batch, seq, MLA_QK_PAD)
    return pl.pallas_call(
        _mla_proj_kernel,
        grid=(batch, tiles),
        in_specs=[
            pl.BlockSpec((TILE, D_MODEL), lambda b, t: (b * tiles + t, 0)),
            full(g), full(wa), full(qn), full(kvn), full(wuq), full(wukv), tab, tab, tab,
        ],
        out_specs=out_specs,
        out_shape=out_shape,
        compiler_params=pltpu.CompilerParams(
            dimension_semantics=("arbitrary", "arbitrary"), vmem_limit_bytes=VMEM_LIMIT),
        name="mla_proj",
    )(x, g, wa, qn, kvn, wuq, wukv, cos, slo, shi)


def _post_attn_kernel(h_ref, o_ref, wo_ref, g_ref, wr_hi_ref, wr_lo_ref, h1_ref, hn_ref, comb_ref):
    h1 = h_ref[...] + jnp.dot(o_ref[...], wo_ref[...], preferred_element_type=F32)
    h1_ref[...] = h1
    hn = _rms(h1, g_ref[...])
    hi = hn.astype(BF16)
    hn_ref[...] = hi
    lo = (hn - hi.astype(F32)).astype(BF16)
    w_hi = wr_hi_ref[...]
    logits = (jnp.dot(hi, w_hi, preferred_element_type=F32)
              + jnp.dot(hi, wr_lo_ref[...], preferred_element_type=F32)
              + jnp.dot(lo, w_hi, preferred_element_type=F32))
    lane_i = lax.broadcasted_iota(jnp.int32, logits.shape, 1)
    lane = lane_i.astype(F32)
    is_group = (lane_i >= N_EXPERTS) & (lane_i < N_EXPERTS + N_GROUPS)

    def first_max(mask):
        v = jnp.max(jnp.where(mask, logits, -jnp.inf), axis=1, keepdims=True)
        idx = jnp.min(jnp.where(mask & (logits == v), lane, 1e9), axis=1, keepdims=True)
        return v, idx

    g_max, g_lane = first_max(is_group)
    g_sum = jnp.sum(jnp.where(is_group, jnp.exp(logits - g_max), 0.0), axis=1, keepdims=True)
    g_w = 1.0 / g_sum
    group_of_lane = (lane_i >> 2).astype(F32)
    in_group = (lane_i < N_EXPERTS) & (group_of_lane == (g_lane - N_EXPERTS))
    v1, i1 = first_max(in_group)
    v2, i2 = first_max(in_group & (lane != i1))
    t = jnp.exp(v2 - v1)
    w1 = g_w / (1.0 + t)
    w2 = w1 * t
    comb_ref[...] = jnp.where(lane == i1, w1, 0.0) + jnp.where(lane == i2, w2, 0.0)


def _post_attn(h, o, wo, g, wr_hi, wr_lo):
    n = h.shape[0]
    tile = POST_TILE
    row = lambda w: pl.BlockSpec((tile, w), lambda t: (t, 0))
    full = lambda a: pl.BlockSpec(a.shape, lambda t: (0,) * a.ndim)
    return pl.pallas_call(
        _post_attn_kernel,
        grid=(n // tile,),
        in_specs=[row(D_MODEL), row(D_MODEL), full(wo), full(g), full(wr_hi), full(wr_lo)],
        out_specs=[row(D_MODEL), row(D_MODEL), row(ROUTER_LANES)],
        out_shape=[
            jax.ShapeDtypeStruct((n, D_MODEL), F32),
            jax.ShapeDtypeStruct((n, D_MODEL), BF16),
            jax.ShapeDtypeStruct((n, ROUTER_LANES), F32),
        ],
        compiler_params=pltpu.CompilerParams(
            dimension_semantics=("arbitrary",), vmem_limit_bytes=VMEM_LIMIT),
        name="post_attn",
    )(h, o, wo, g, wr_hi, wr_lo)


def _moe_ple_kernel(hn_ref, comb_ref, h1_ref, p_ref, wg_ref, wu_ref, wd_ref, pn_ref, pg_ref, pp_ref,
                    fn_ref, out_ref, acc_ref, *, final):
    grp = pl.program_id(1)
    x = hn_ref[...]
    a = jnp.dot(x, wg_ref[0], preferred_element_type=F32)
    u = jnp.dot(x, wu_ref[0], preferred_element_type=F32)
    act = a * jax.nn.sigmoid(a) * u
    comb = comb_ref[...]
    lane = lax.broadcasted_iota(jnp.int32, comb.shape, 1)
    parts = []
    for j in range(PER_GROUP):
        c = jnp.sum(jnp.where(lane == grp * PER_GROUP + j, comb, 0.0), axis=1, keepdims=True)
        parts.append((act[:, j * EXPERT_FF:(j + 1) * EXPERT_FF] * c).astype(BF16))
    y = jnp.dot(jnp.concatenate(parts, axis=1), wd_ref[0], preferred_element_type=F32)

    @pl.when(grp == 0)
    def _():
        acc_ref[...] = y

    @pl.when(grp > 0)
    def _():
        acc_ref[...] += y

    @pl.when(grp == N_GROUPS - 1)
    def _():
        h2 = h1_ref[...] + acc_ref[...]
        gate = jax.nn.sigmoid(
            jnp.dot(_rms(h2, pn_ref[...]).astype(BF16), pg_ref[...], preferred_element_type=F32))
        proj = jnp.dot(p_ref[...].astype(BF16), pp_ref[...], preferred_element_type=F32)
        h3 = h2 + gate * proj
        out_ref[...] = _rms(h3, fn_ref[...]) if final else h3


def _moe_ple(hn, comb, h1, p, wg, wu, wd, pn, pg, pp, fn, final):
    n = hn.shape[0]
    tile = POST_TILE
    row = lambda w: pl.BlockSpec((tile, w), lambda t, e: (t, 0))
    full = lambda a: pl.BlockSpec(a.shape, lambda t, e: (0,) * a.ndim)
    grp = lambda a: pl.BlockSpec((1,) + a.shape[1:], lambda t, e: (e, 0, 0))
    return pl.pallas_call(
        functools.partial(_moe_ple_kernel, final=final),
        grid=(n // tile, N_GROUPS),
        in_specs=[row(D_MODEL), row(ROUTER_LANES), row(D_MODEL), row(PLE_DIM),
                  grp(wg), grp(wu), grp(wd), full(pn), full(pg), full(pp), full(fn)],
        out_specs=row(D_MODEL),
        out_shape=jax.ShapeDtypeStruct((n, D_MODEL), F32),
        scratch_shapes=[pltpu.VMEM((tile, D_MODEL), F32)],
        compiler_params=pltpu.CompilerParams(
            dimension_semantics=("arbitrary", "arbitrary"), vmem_limit_bytes=VMEM_LIMIT),
        name="moe_ple",
    )(hn, comb, h1, p, wg, wu, wd, pn, pg, pp, fn)


def _rope_tables(seq, half):
    inv = THETA ** (-jnp.arange(half, dtype=F32) / half)
    ang = jnp.arange(seq, dtype=F32)[:, None] * inv[None, :]
    return jnp.cos(ang), jnp.sin(ang)


def _split_bf16(w):
    hi = w.astype(BF16)
    return hi, (w - hi.astype(F32)).astype(BF16)


def kernel(x, p, attn_norm, ffn_norm, ple_norm, final_norm, moba_wqkv, moba_wo, mla_wdq, mla_qnorm,
           mla_wuq, mla_wdkv, mla_kvnorm, mla_wukv, mla_wo, moe_wgroup, moe_wexpert, moe_wgate,
           moe_wup, moe_wdown, ple_gate, ple_proj):
    batch, seq, d = x.shape
    n = batch * seq
    h = x.reshape(n, d)

    cos64, sin64 = _rope_tables(seq, HEAD_DIM // 2)
    moba_cos = jnp.concatenate([cos64, cos64], axis=1)
    moba_sin = jnp.concatenate([-sin64, sin64], axis=1)
    cos32, sin32 = _rope_tables(seq, MLA_ROPE // 2)
    z32 = jnp.zeros_like(sin32)
    mla_cos = jnp.concatenate([cos32] * 4, axis=1)
    mla_slo = jnp.concatenate([-sin32, z32, -sin32, z32], axis=1)
    mla_shi = jnp.concatenate([z32, sin32, z32, sin32], axis=1)

    row = lambda v: v.reshape(1, -1)

    for layer in range(N_LAYERS):
        j = layer // 2
        if layer % 2 == 0:
            q, k, vt = _moba_proj(h, row(attn_norm[layer]), moba_wqkv[j].astype(BF16),
                                  moba_cos, moba_sin, batch, seq)
            o = _moba_attn(q, k, vt)
            wo = moba_wo[j]
        else:
            wuq = mla_wuq[j].reshape(MLA_Q_RANK, HEADS, MLA_NOPE + MLA_ROPE)
            wuq = jnp.concatenate([wuq[:, :, :MLA_NOPE].reshape(MLA_Q_RANK, -1),
                                   wuq[:, :, MLA_NOPE:].reshape(MLA_Q_RANK, -1)], axis=1)
            wukv = mla_wukv[j].reshape(MLA_KV_RANK, HEADS, MLA_NOPE + HEAD_DIM)
            wukv = jnp.concatenate([wukv[:, :, :MLA_NOPE].reshape(MLA_KV_RANK, -1),
                                    wukv[:, :, MLA_NOPE:].reshape(MLA_KV_RANK, -1)], axis=1)
            wa = jnp.concatenate(
                [mla_wdq[j], mla_wdkv[j], jnp.zeros((d, 128 - MLA_ROPE), F32)], axis=1)
            q, k, vt = _mla_proj(h, row(attn_norm[layer]), wa.astype(BF16), row(mla_qnorm[j]),
                                 row(mla_kvnorm[j]), wuq.astype(BF16), wukv.astype(BF16),
                                 mla_cos, mla_slo, mla_shi, batch, seq)
            o = _causal_attn(q, k, vt)
            wo = mla_wo[j]

        wr = jnp.concatenate(
            [moe_wexpert[layer], moe_wgroup[layer],
             jnp.zeros((d, ROUTER_LANES - N_EXPERTS - N_GROUPS), F32)], axis=1)
        wr_hi, wr_lo = _split_bf16(wr)
        h1, hn, comb = _post_attn(h, o.reshape(n, -1), wo.astype(BF16), row(ffn_norm[layer]), wr_hi, wr_lo)

        def by_group(w):
            w = w.reshape(N_GROUPS, PER_GROUP, d, EXPERT_FF).transpose(0, 2, 1, 3)
            return w.reshape(N_GROUPS, d, PER_GROUP * EXPERT_FF).astype(BF16)

        wd = moe_wdown[layer].reshape(N_GROUPS, PER_GROUP * EXPERT_FF, d).astype(BF16)
        h = _moe_ple(hn, comb, h1, p[layer].reshape(n, PLE_DIM), by_group(moe_wgate[layer]),
                     by_group(moe_wup[layer]), wd, row(ple_norm[layer]), ple_gate[layer].astype(BF16),
                     ple_proj[layer].astype(BF16), row(final_norm), layer == N_LAYERS - 1)

    return h.reshape(batch, seq, d)
```

```python
import functools

import jax
import jax.numpy as jnp
from jax import lax
from jax.experimental import pallas as pl
from jax.experimental.pallas import tpu as pltpu

F32 = jnp.float32
BF16 = jnp.bfloat16

D_MODEL = 1024
N_LAYERS = 2
EPS = 1e-6
THETA = 10000.0
NEG = -1e30
LOG2E = 1.4426950408889634

HEADS = 8
HEAD_DIM = 128
MOBA_BLOCK = 256
MOBA_TOPK = 3

MLA_Q_RANK = 384
MLA_KV_RANK = 256
MLA_NOPE = 128
MLA_ROPE = 64
MLA_QK_PAD = 256

N_GROUPS = 4
PER_GROUP = 4
N_EXPERTS = 16
EXPERT_FF = 256
ROUTER_LANES = 128

PLE_DIM = 256

TILE = 512
Q_TILE = 1024
V_ROWS = HEAD_DIM + 16
ATTN_HEADS = 2
POST_TILE = 512

VMEM_LIMIT = 56 * 1024 * 1024


def _rms(x, g):
    return x * lax.rsqrt(jnp.mean(x * x, axis=-1, keepdims=True) + EPS) * g


def _nt_dot(a, b):
    return lax.dot_general(a, b, (((1,), (1,)), ((), ())), preferred_element_type=F32)


def _moba_proj_kernel(x_ref, g_ref, w_ref, cos_ref, sin_ref, qt_ref, k_ref, vt_ref):
    hn = _rms(x_ref[...], g_ref[...])
    qkv = jnp.dot(hn.astype(BF16), w_ref[...], preferred_element_type=F32)
    cos = cos_ref[...]
    sin = sin_ref[...]
    scale = HEAD_DIM ** -0.5 * LOG2E
    hd = HEADS * HEAD_DIM
    for h in range(HEADS):
        lo = h * HEAD_DIM
        qh = qkv[:, lo:lo + HEAD_DIM]
        qh = qh * cos + pltpu.roll(qh, HEAD_DIM // 2, 1) * sin
        qt_ref[0, h] = (qh * scale).T.astype(BF16)
        kh = qkv[:, hd + lo:hd + lo + HEAD_DIM]
        kh = (kh * cos + pltpu.roll(kh, HEAD_DIM // 2, 1) * sin).astype(BF16)
        vh = qkv[:, 2 * hd + lo:2 * hd + lo + HEAD_DIM]
        _store_kv(k_ref, vt_ref, h, kh, vh)


def _store_kv(k_ref, vt_ref, h, kh, vh):
    k_ref[0, h, 0] = kh
    vt_ref[0, h, 0] = jnp.concatenate(
        [vh.T, jnp.ones((V_ROWS - HEAD_DIM, TILE), F32)], axis=0).astype(BF16)


def _qkv_out(batch, seq, dqk):
    steps = seq // TILE
    specs = [
        pl.BlockSpec((1, HEADS, dqk, TILE), lambda b, t: (b, 0, 0, t)),
        pl.BlockSpec((1, HEADS, 1, TILE, dqk), lambda b, t: (b, 0, t, 0, 0)),
        pl.BlockSpec((1, HEADS, 1, V_ROWS, TILE), lambda b, t: (b, 0, t, 0, 0)),
    ]
    shapes = [
        jax.ShapeDtypeStruct((batch, HEADS, dqk, seq), BF16),
        jax.ShapeDtypeStruct((batch, HEADS, steps, TILE, dqk), BF16),
        jax.ShapeDtypeStruct((batch, HEADS, steps, V_ROWS, TILE), BF16),
    ]
    return specs, shapes


def _moba_proj(x, g, w, cos2, sin2, batch, seq):
    tiles = seq // TILE
    out_specs, out_shape = _qkv_out(batch, seq, HEAD_DIM)
    return pl.pallas_call(
        _moba_proj_kernel,
        grid=(batch, tiles),
        in_specs=[
            pl.BlockSpec((TILE, D_MODEL), lambda b, t: (b * tiles + t, 0)),
            pl.BlockSpec((1, D_MODEL), lambda b, t: (0, 0)),
            pl.BlockSpec((D_MODEL, 3 * HEADS * HEAD_DIM), lambda b, t: (0, 0)),
            pl.BlockSpec((TILE, HEAD_DIM), lambda b, t: (t, 0)),
            pl.BlockSpec((TILE, HEAD_DIM), lambda b, t: (t, 0)),
        ],
        out_specs=out_specs,
        out_shape=out_shape,
        compiler_params=pltpu.CompilerParams(
            dimension_semantics=("arbitrary", "arbitrary"), vmem_limit_bytes=VMEM_LIMIT),
        name="moba_proj",
    )(x, g, w, cos2, sin2)


def _attn_scratch(nh):
    return ([pltpu.VMEM((TILE, Q_TILE), F32)] * (2 * nh) + [pltpu.VMEM((1, Q_TILE), F32)] * (2 * nh)
            + [pltpu.VMEM((1, Q_TILE), F32)] * nh + [pltpu.VMEM((V_ROWS, Q_TILE), F32)] * nh)


def _attention(qt_ref, k_ref, vt_ref, o_ref, scratch, i, bias_ref=None):
    nh, _, tq = qt_ref.shape[1:]
    per_q = tq // TILE
    assert per_q % 2 == 0
    s_refs = (scratch[:nh], scratch[nh:2 * nh])
    smax_refs = (scratch[2 * nh:3 * nh], scratch[3 * nh:4 * nh])
    m_refs = scratch[4 * nh:5 * nh]
    acc_refs = scratch[5 * nh:6 * nh]

    def produce(h, step, par):
        s = jnp.dot(k_ref[0, h, step], qt_ref[0, h], preferred_element_type=F32)
        if bias_ref is not None:
            nb = TILE // MOBA_BLOCK
            b = bias_ref[h, pl.ds(step * nb, nb)]
            s = (s.reshape(nb, MOBA_BLOCK, tq) + b).reshape(TILE, tq)
        s_refs[par][h][...] = s
        smax_refs[par][h][...] = jnp.max(s, axis=0, keepdims=True)

    def consume(h, step, par, diagonal=None):
        s = s_refs[par][h][...]
        if diagonal is not None:
            kpos = diagonal * TILE + lax.broadcasted_iota(jnp.int32, s.shape, 0)
            qpos = lax.broadcasted_iota(jnp.int32, s.shape, 1)
            s = jnp.where(kpos <= qpos, s, NEG)
            s_max = jnp.max(s, axis=0, keepdims=True)
        else:
            s_max = smax_refs[par][h][...]
        m = m_refs[h][...]
        m_new = jnp.maximum(m, s_max)
        m_refs[h][...] = m_new
        p = jnp.exp2(s - m_new).astype(BF16)
        pv = jnp.dot(vt_ref[0, h, step], p, preferred_element_type=F32)
        acc_refs[h][...] = jnp.exp2(m - m_new) * acc_refs[h][...] + pv

    for h in range(nh):
        m_refs[h][...] = jnp.full((1, tq), NEG, F32)
        acc_refs[h][...] = jnp.zeros((V_ROWS, tq), F32)
        produce(h, 0, 0)

    def body(t, _):
        for par in range(2):
            @pl.when((t & 1) == par)
            def _():
                for h in range(nh):
                    produce(h, t + 1, 1 - par)
                    consume(h, t, par)
        return 0

    lax.fori_loop(0, per_q * i, body, 0)
    for d in range(per_q):
        for h in range(nh):
            if d + 1 < per_q:
                produce(h, per_q * i + d + 1, (d + 1) & 1)
            consume(h, per_q * i + d, d & 1, diagonal=d)

    for h in range(nh):
        acc = acc_refs[h][...]
        out = acc[:HEAD_DIM] / acc[HEAD_DIM:HEAD_DIM + 1]
        o_ref[0, :, h * HEAD_DIM:(h + 1) * HEAD_DIM] = out.T.astype(BF16)


def _moba_attn_kernel(qt_ref, k_ref, vt_ref, o_ref, kmean_ref, bias_ref, *scratch):
    i = pl.program_id(2)
    nh, _, tq = qt_ref.shape[1:]
    steps = k_ref.shape[2]
    nb = TILE // MOBA_BLOCK
    nblk = steps * nb

    @pl.when(i == 0)
    def _():
        for h in range(nh):
            for n in range(nblk):
                kb = k_ref[0, h, n // nb, (n % nb) * MOBA_BLOCK:(n % nb + 1) * MOBA_BLOCK, :]
                kmean_ref[h, n:n + 1, :] = jnp.mean(kb.astype(F32), axis=0, keepdims=True)

    nidx = lax.broadcasted_iota(jnp.int32, (nblk, tq), 0)
    own = (i * tq + lax.broadcasted_iota(jnp.int32, (nblk, tq), 1)) >> 8
    past = nidx < own
    for h in range(nh):
        qt = qt_ref[0, h]
        km = kmean_ref[h]
        km_hi = km.astype(BF16)
        km_lo = (km - km_hi.astype(F32)).astype(BF16)
        gate = (jnp.dot(km_hi, qt, preferred_element_type=F32)
                + jnp.dot(km_lo, qt, preferred_element_type=F32))
        g1 = jnp.where(past, gate, -jnp.inf)
        thr = jnp.max(g1, axis=0, keepdims=True)
        g = g1
        for _ in range(MOBA_TOPK - 1):
            g = jnp.where(g >= thr, -jnp.inf, g)
            thr = jnp.max(g, axis=0, keepdims=True)
        keep = (past & (g1 >= thr)) | (nidx == own)
        bias = jnp.where(keep, 0.0, NEG)
        for n in range(nblk):
            bias_ref[h, n] = bias[n:n + 1, :]

    _attention(qt_ref, k_ref, vt_ref, o_ref, scratch, i, bias_ref)


def _causal_attn_kernel(qt_ref, k_ref, vt_ref, o_ref, *scratch):
    _attention(qt_ref, k_ref, vt_ref, o_ref, scratch, pl.program_id(2))


def _attn_call(kernel_fn, name, qt, k, vt, scratch):
    batch, heads, dqk, seq = qt.shape
    steps = seq // TILE
    nh = ATTN_HEADS
    return pl.pallas_call(
        kernel_fn,
        grid=(batch, heads // nh, seq // Q_TILE),
        in_specs=[
            pl.BlockSpec((1, nh, dqk, Q_TILE), lambda b, h, i: (b, h, 0, i)),
            pl.BlockSpec((1, nh, steps, TILE, dqk), lambda b, h, i: (b, h, 0, 0, 0)),
            pl.BlockSpec((1, nh, steps, V_ROWS, TILE), lambda b, h, i: (b, h, 0, 0, 0)),
        ],
        out_specs=pl.BlockSpec((1, Q_TILE, nh * HEAD_DIM), lambda b, h, i: (b, i, h)),
        out_shape=jax.ShapeDtypeStruct((batch, seq, heads * HEAD_DIM), BF16),
        scratch_shapes=scratch + _attn_scratch(nh),
        compiler_params=pltpu.CompilerParams(
            dimension_semantics=("arbitrary", "arbitrary", "arbitrary"),
            vmem_limit_bytes=VMEM_LIMIT),
        name=name,
    )(qt, k, vt)


def _moba_attn(qt, k, vt):
    seq = qt.shape[3]
    scratch = [pltpu.VMEM((ATTN_HEADS, seq // MOBA_BLOCK, HEAD_DIM), F32),
               pltpu.VMEM((ATTN_HEADS, seq // MOBA_BLOCK, 1, Q_TILE), F32)]
    return _attn_call(_moba_attn_kernel, "moba_attn", qt, k, vt, scratch)


def _causal_attn(qt, k, vt):
    return _attn_call(_causal_attn_kernel, "mla_attn", qt, k, vt, [])


def _rope_pairs32(x, cos, sin_lo, sin_hi):
    n = x.shape[1]
    return x * cos + pltpu.roll(x, n - 32, 1) * sin_lo + pltpu.roll(x, 32, 1) * sin_hi


def _mla_proj_kernel(x_ref, g_ref, wa_ref, qn_ref, kvn_ref, wuq_ref, wukv_ref,
                     cos_ref, slo_ref, shi_ref, qt_ref, k_ref, vt_ref):
    hn = _rms(x_ref[...], g_ref[...])
    r = jnp.dot(hn.astype(BF16), wa_ref[...], preferred_element_type=F32)
    cq = _rms(r[:, :MLA_Q_RANK], qn_ref[...])
    ckv = _rms(r[:, MLA_Q_RANK:MLA_Q_RANK + MLA_KV_RANK], kvn_ref[...])
    kr = r[:, MLA_Q_RANK + MLA_KV_RANK:]
    q = jnp.dot(cq.astype(BF16), wuq_ref[...], preferred_element_type=F32)
    kv = jnp.dot(ckv.astype(BF16), wukv_ref[...], preferred_element_type=F32)
    cos = cos_ref[...]
    slo = slo_ref[...]
    shi = shi_ref[...]
    nn = HEADS * MLA_NOPE
    reps = HEADS * MLA_ROPE // 128
    qr = _rope_pairs32(q[:, nn:], jnp.concatenate([cos] * reps, axis=1),
                       jnp.concatenate([slo] * reps, axis=1), jnp.concatenate([shi] * reps, axis=1))
    kr = _rope_pairs32(kr, cos, slo, shi)
    scale = (MLA_NOPE + MLA_ROPE) ** -0.5 * LOG2E
    low_half = lax.broadcasted_iota(jnp.int32, kr.shape, 1) < MLA_ROPE
    for h in range(HEADS):
        pair = qr[:, (h // 2) * 128:(h // 2 + 1) * 128]
        if h % 2:
            pair = pltpu.roll(pair, MLA_ROPE, 1)
        qh = jnp.concatenate(
            [q[:, h * MLA_NOPE:(h + 1) * MLA_NOPE], jnp.where(low_half, pair, 0.0)], axis=1)
        qt_ref[0, h] = (qh * scale).T.astype(BF16)
        kh = jnp.concatenate([kv[:, h * MLA_NOPE:(h + 1) * MLA_NOPE], kr], axis=1).astype(BF16)
        vh = kv[:, nn + h * HEAD_DIM:nn + (h + 1) * HEAD_DIM]
        _store_kv(k_ref, vt_ref, h, kh, vh)


def _mla_proj(x, g, wa, qn, kvn, wuq, wukv, cos, slo, shi, batch, seq):
    tiles = seq // TILE
    full = lambda a: pl.BlockSpec(a.shape, lambda b, t: (0,) * a.ndim)
    tab = pl.BlockSpec((TILE, 128), lambda b, t: (t, 0))
    out_specs, out_shape = _qkv_out(batch, seq, MLA_QK_PAD)
    return pl.pallas_call(
        _mla_proj_kernel,
        grid=(batch, tiles),
        in_specs=[
            pl.BlockSpec((TILE, D_MODEL), lambda b, t: (b * tiles + t, 0)),
            full(g), full(wa), full(qn), full(kvn), full(wuq), full(wukv), tab, tab, tab,
        ],
        out_specs=out_specs,
        out_shape=out_shape,
        compiler_params=pltpu.CompilerParams(
            dimension_semantics=("arbitrary", "arbitrary"), vmem_limit_bytes=VMEM_LIMIT),
        name="mla_proj",
    )(x, g, wa, qn, kvn, wuq, wukv, cos, slo, shi)


def _post_attn_kernel(h_ref, o_ref, wo_ref, g_ref, wr_hi_ref, wr_lo_ref, h1_ref, hn_ref, comb_ref):
    h1 = h_ref[...] + jnp.dot(o_ref[...], wo_ref[...], preferred_element_type=F32)
    h1_ref[...] = h1
    hn = _rms(h1, g_ref[...])
    hi = hn.astype(BF16)
    hn_ref[...] = hi
    lo = (hn - hi.astype(F32)).astype(BF16)
    w_hi = wr_hi_ref[...]
    logits = (jnp.dot(hi, w_hi, preferred_element_type=F32)
              + jnp.dot(hi, wr_lo_ref[...], preferred_element_type=F32)
              + jnp.dot(lo, w_hi, preferred_element_type=F32))
    lane_i = lax.broadcasted_iota(jnp.int32, logits.shape, 1)
    lane = lane_i.astype(F32)
    is_group = (lane_i >= N_EXPERTS) & (lane_i < N_EXPERTS + N_GROUPS)

    def first_max(mask):
        v = jnp.max(jnp.where(mask, logits, -jnp.inf), axis=1, keepdims=True)
        idx = jnp.min(jnp.where(mask & (logits == v), lane, 1e9), axis=1, keepdims=True)
        return v, idx

    g_max, g_lane = first_max(is_group)
    g_sum = jnp.sum(jnp.where(is_group, jnp.exp(logits - g_max), 0.0), axis=1, keepdims=True)
    g_w = 1.0 / g_sum
    group_of_lane = (lane_i >> 2).astype(F32)
    in_group = (lane_i < N_EXPERTS) & (group_of_lane == (g_lane - N_EXPERTS))
    v1, i1 = first_max(in_group)
    v2, i2 = first_max(in_group & (lane != i1))
    t = jnp.exp(v2 - v1)
    w1 = g_w / (1.0 + t)
    w2 = w1 * t
    comb_ref[...] = jnp.where(lane == i1, w1, 0.0) + jnp.where(lane == i2, w2, 0.0)


def _post_attn(h, o, wo, g, wr_hi, wr_lo):
    n = h.shape[0]
    tile = POST_TILE
    row = lambda w: pl.BlockSpec((tile, w), lambda t: (t, 0))
    full = lambda a: pl.BlockSpec(a.shape, lambda t: (0,) * a.ndim)
    return pl.pallas_call(
        _post_attn_kernel,
        grid=(n // tile,),
        in_specs=[row(D_MODEL), row(D_MODEL), full(wo), full(g), full(wr_hi), full(wr_lo)],
        out_specs=[row(D_MODEL), row(D_MODEL), row(ROUTER_LANES)],
        out_shape=[
            jax.ShapeDtypeStruct((n, D_MODEL), F32),
            jax.ShapeDtypeStruct((n, D_MODEL), BF16),
            jax.ShapeDtypeStruct((n, ROUTER_LANES), F32),
        ],
        compiler_params=pltpu.CompilerParams(
            dimension_semantics=("arbitrary",), vmem_limit_bytes=VMEM_LIMIT),
        name="post_attn",
    )(h, o, wo, g, wr_hi, wr_lo)


def _moe_ple_kernel(hn_ref, comb_ref, h1_ref, p_ref, wg_ref, wu_ref, wd_ref, pn_ref, pg_ref, pp_ref,
                    fn_ref, out_ref, acc_ref, *, final):
    grp = pl.program_id(1)
    x = hn_ref[...]
    a = jnp.dot(x, wg_ref[0], preferred_element_type=F32)
    u = jnp.dot(x, wu_ref[0], preferred_element_type=F32)
    act = a * jax.nn.sigmoid(a) * u
    comb = comb_ref[...]
    lane = lax.broadcasted_iota(jnp.int32, comb.shape, 1)
    parts = []
    for j in range(PER_GROUP):
        c = jnp.sum(jnp.where(lane == grp * PER_GROUP + j, comb, 0.0), axis=1, keepdims=True)
        parts.append((act[:, j * EXPERT_FF:(j + 1) * EXPERT_FF] * c).astype(BF16))
    y = jnp.dot(jnp.concatenate(parts, axis=1), wd_ref[0], preferred_element_type=F32)

    @pl.when(grp == 0)
    def _():
        acc_ref[...] = y

    @pl.when(grp > 0)
    def _():
        acc_ref[...] += y

    @pl.when(grp == N_GROUPS - 1)
    def _():
        h2 = h1_ref[...] + acc_ref[...]
        gate = jax.nn.sigmoid(
            jnp.dot(_rms(h2, pn_ref[...]).astype(BF16), pg_ref[...], preferred_element_type=F32))
        proj = jnp.dot(p_ref[...].astype(BF16), pp_ref[...], preferred_element_type=F32)
        h3 = h2 + gate * proj
        out_ref[...] = _rms(h3, fn_ref[...]) if final else h3


def _moe_ple(hn, comb, h1, p, wg, wu, wd, pn, pg, pp, fn, final):
    n = hn.shape[0]
    tile = POST_TILE
    row = lambda w: pl.BlockSpec((tile, w), lambda t, e: (t, 0))
    full = lambda a: pl.BlockSpec(a.shape, lambda t, e: (0,) * a.ndim)
    grp = lambda a: pl.BlockSpec((1,) + a.shape[1:], lambda t, e: (e, 0, 0))
    return pl.pallas_call(
        functools.partial(_moe_ple_kernel, final=final),
        grid=(n // tile, N_GROUPS),
        in_specs=[row(D_MODEL), row(ROUTER_LANES), row(D_MODEL), row(PLE_DIM),
                  grp(wg), grp(wu), grp(wd), full(pn), full(pg), full(pp), full(fn)],
        out_specs=row(D_MODEL),
        out_shape=jax.ShapeDtypeStruct((n, D_MODEL), F32),
        scratch_shapes=[pltpu.VMEM((tile, D_MODEL), F32)],
        compiler_params=pltpu.CompilerParams(
            dimension_semantics=("arbitrary", "arbitrary"), vmem_limit_bytes=VMEM_LIMIT),
        name="moe_ple",
    )(hn, comb, h1, p, wg, wu, wd, pn, pg, pp, fn)


def _rope_tables(seq, half):
    inv = THETA ** (-jnp.arange(half, dtype=F32) / half)
    ang = jnp.arange(seq, dtype=F32)[:, None] * inv[None, :]
    return jnp.cos(ang), jnp.sin(ang)


def _split_bf16(w):
    hi = w.astype(BF16)
    return hi, (w - hi.astype(F32)).astype(BF16)


def kernel(x, p, attn_norm, ffn_norm, ple_norm, final_norm, moba_wqkv, moba_wo, mla_wdq, mla_qnorm,
           mla_wuq, mla_wdkv, mla_kvnorm, mla_wukv, mla_wo, moe_wgroup, moe_wexpert, moe_wgate,
           moe_wup, moe_wdown, ple_gate, ple_proj):
    batch, seq, d = x.shape
    n = batch * seq
    h = x.reshape(n, d)

    cos64, sin64 = _rope_tables(seq, HEAD_DIM // 2)
    moba_cos = jnp.concatenate([cos64, cos64], axis=1)
    moba_sin = jnp.concatenate([-sin64, sin64], axis=1)
    cos32, sin32 = _rope_tables(seq, MLA_ROPE // 2)
    z32 = jnp.zeros_like(sin32)
    mla_cos = jnp.concatenate([cos32] * 4, axis=1)
    mla_slo = jnp.concatenate([-sin32, z32, -sin32, z32], axis=1)
    mla_shi = jnp.concatenate([z32, sin32, z32, sin32], axis=1)

    row = lambda v: v.reshape(1, -1)

    for layer in range(N_LAYERS):
        j = layer // 2
        if layer % 2 == 0:
            q, k, vt = _moba_proj(h, row(attn_norm[layer]), moba_wqkv[j].astype(BF16),
                                  moba_cos, moba_sin, batch, seq)
            o = _moba_attn(q, k, vt)
            wo = moba_wo[j]
        else:
            wuq = mla_wuq[j].reshape(MLA_Q_RANK, HEADS, MLA_NOPE + MLA_ROPE)
            wuq = jnp.concatenate([wuq[:, :, :MLA_NOPE].reshape(MLA_Q_RANK, -1),
                                   wuq[:, :, MLA_NOPE:].reshape(MLA_Q_RANK, -1)], axis=1)
            wukv = mla_wukv[j].reshape(MLA_KV_RANK, HEADS, MLA_NOPE + HEAD_DIM)
            wukv = jnp.concatenate([wukv[:, :, :MLA_NOPE].reshape(MLA_KV_RANK, -1),
                                    wukv[:, :, MLA_NOPE:].reshape(MLA_KV_RANK, -1)], axis=1)
            wa = jnp.concatenate(
                [mla_wdq[j], mla_wdkv[j], jnp.zeros((d, 128 - MLA_ROPE), F32)], axis=1)
            q, k, vt = _mla_proj(h, row(attn_norm[layer]), wa.astype(BF16), row(mla_qnorm[j]),
                                 row(mla_kvnorm[j]), wuq.astype(BF16), wukv.astype(BF16),
                                 mla_cos, mla_slo, mla_shi, batch, seq)
            o = _causal_attn(q, k, vt)
            wo = mla_wo[j]

        wr = jnp.concatenate(
            [moe_wexpert[layer], moe_wgroup[layer],
             jnp.zeros((d, ROUTER_LANES - N_EXPERTS - N_GROUPS), F32)], axis=1)
        wr_hi, wr_lo = _split_bf16(wr)
        h1, hn, comb = _post_attn(h, o.reshape(n, -1), wo.astype(BF16), row(ffn_norm[layer]), wr_hi, wr_lo)

        def by_group(w):
            w = w.reshape(N_GROUPS, PER_GROUP, d, EXPERT_FF).transpose(0, 2, 1, 3)
            return w.reshape(N_GROUPS, d, PER_GROUP * EXPERT_FF).astype(BF16)

        wd = moe_wdown[layer].reshape(N_GROUPS, PER_GROUP * EXPERT_FF, d).astype(BF16)
        h = _moe_ple(hn, comb, h1, p[layer].reshape(n, PLE_DIM), by_group(moe_wgate[layer]),
                     by_group(moe_wup[layer]), wd, row(ple_norm[layer]), ple_gate[layer].astype(BF16),
                     ple_proj[layer].astype(BF16), row(final_norm), layer == N_LAYERS - 1)

    return h.reshape(batch, seq, d)
```

```python
import functools

import jax
import jax.numpy as jnp
from jax import lax
from jax.experimental import pallas as pl
from jax.experimental.pallas import tpu as pltpu

F32 = jnp.float32
BF16 = jnp.bfloat16

D_MODEL = 1024
N_LAYERS = 2
EPS = 1e-6
THETA = 10000.0
NEG = -1e30
LOG2E = 1.4426950408889634

HEADS = 8
HEAD_DIM = 128
MOBA_BLOCK = 256
MOBA_TOPK = 3

MLA_Q_RANK = 384
MLA_KV_RANK = 256
MLA_NOPE = 128
MLA_ROPE = 64
MLA_QK_PAD = 256

N_GROUPS = 4
PER_GROUP = 4
N_EXPERTS = 16
EXPERT_FF = 256
ROUTER_LANES = 128

PLE_DIM = 256

TILE = 512
Q_TILE = 1024
V_ROWS = HEAD_DIM + 16
ATTN_HEADS = 2
POST_TILE = 512
SORT_ALIGN = 16
SORT_ROWS = 640
MOE_CHUNK = 160

VMEM_LIMIT = 56 * 1024 * 1024


def _rms(x, g):
    return x * lax.rsqrt(jnp.mean(x * x, axis=-1, keepdims=True) + EPS) * g


def _nt_dot(a, b):
    return lax.dot_general(a, b, (((1,), (1,)), ((), ())), preferred_element_type=F32)


def _moba_proj_kernel(x_ref, g_ref, w_ref, cos_ref, sin_ref, qt_ref, k_ref, vt_ref):
    hn = _rms(x_ref[...], g_ref[...])
    qkv = jnp.dot(hn.astype(BF16), w_ref[...], preferred_element_type=F32)
    cos = cos_ref[...]
    sin = sin_ref[...]
    scale = HEAD_DIM ** -0.5 * LOG2E
    hd = HEADS * HEAD_DIM
    for h in range(HEADS):
        lo = h * HEAD_DIM
        qh = qkv[:, lo:lo + HEAD_DIM]
        qh = qh * cos + pltpu.roll(qh, HEAD_DIM // 2, 1) * sin
        qt_ref[0, h] = (qh * scale).T.astype(BF16)
        kh = qkv[:, hd + lo:hd + lo + HEAD_DIM]
        kh = (kh * cos + pltpu.roll(kh, HEAD_DIM // 2, 1) * sin).astype(BF16)
        vh = qkv[:, 2 * hd + lo:2 * hd + lo + HEAD_DIM]
        _store_kv(k_ref, vt_ref, h, kh, vh)


def _store_kv(k_ref, vt_ref, h, kh, vh):
    k_ref[0, h, 0] = kh
    vt_ref[0, h, 0] = jnp.concatenate(
        [vh.T, jnp.ones((V_ROWS - HEAD_DIM, TILE), F32)], axis=0).astype(BF16)


def _qkv_out(batch, seq, dqk):
    steps = seq // TILE
    specs = [
        pl.BlockSpec((1, HEADS, dqk, TILE), lambda b, t: (b, 0, 0, t)),
        pl.BlockSpec((1, HEADS, 1, TILE, dqk), lambda b, t: (b, 0, t, 0, 0)),
        pl.BlockSpec((1, HEADS, 1, V_ROWS, TILE), lambda b, t: (b, 0, t, 0, 0)),
    ]
    shapes = [
        jax.ShapeDtypeStruct((batch, HEADS, dqk, seq), BF16),
        jax.ShapeDtypeStruct((batch, HEADS, steps, TILE, dqk), BF16),
        jax.ShapeDtypeStruct((batch, HEADS, steps, V_ROWS, TILE), BF16),
    ]
    return specs, shapes


def _moba_proj(x, g, w, cos2, sin2, batch, seq):
    tiles = seq // TILE
    out_specs, out_shape = _qkv_out(batch, seq, HEAD_DIM)
    return pl.pallas_call(
        _moba_proj_kernel,
        grid=(batch, tiles),
        in_specs=[
            pl.BlockSpec((TILE, D_MODEL), lambda b, t: (b * tiles + t, 0)),
            pl.BlockSpec((1, D_MODEL), lambda b, t: (0, 0)),
            pl.BlockSpec((D_MODEL, 3 * HEADS * HEAD_DIM), lambda b, t: (0, 0)),
            pl.BlockSpec((TILE, HEAD_DIM), lambda b, t: (t, 0)),
            pl.BlockSpec((TILE, HEAD_DIM), lambda b, t: (t, 0)),
        ],
        out_specs=out_specs,
        out_shape=out_shape,
        compiler_params=pltpu.CompilerParams(
            dimension_semantics=("arbitrary", "arbitrary"), vmem_limit_bytes=VMEM_LIMIT),
        name="moba_proj",
    )(x, g, w, cos2, sin2)


def _attn_scratch(nh):
    return ([pltpu.VMEM((TILE, Q_TILE), F32)] * (2 * nh) + [pltpu.VMEM((1, Q_TILE), F32)] * (2 * nh)
            + [pltpu.VMEM((1, Q_TILE), F32)] * nh + [pltpu.VMEM((V_ROWS, Q_TILE), F32)] * nh)


def _attention(qt_ref, k_ref, vt_ref, o_ref, scratch, i, bias_ref=None):
    nh, _, tq = qt_ref.shape[1:]
    per_q = tq // TILE
    assert per_q % 2 == 0
    s_refs = (scratch[:nh], scratch[nh:2 * nh])
    smax_refs = (scratch[2 * nh:3 * nh], scratch[3 * nh:4 * nh])
    m_refs = scratch[4 * nh:5 * nh]
    acc_refs = scratch[5 * nh:6 * nh]

    def produce(h, step, par):
        s = jnp.dot(k_ref[0, h, step], qt_ref[0, h], preferred_element_type=F32)
        if bias_ref is not None:
            nb = TILE // MOBA_BLOCK
            b = bias_ref[h, pl.ds(step * nb, nb)]
            s = (s.reshape(nb, MOBA_BLOCK, tq) + b).reshape(TILE, tq)
        s_refs[par][h][...] = s
        smax_refs[par][h][...] = jnp.max(s, axis=0, keepdims=True)

    def consume(h, step, par, diagonal=None):
        s = s_refs[par][h][...]
        if diagonal is not None:
            kpos = diagonal * TILE + lax.broadcasted_iota(jnp.int32, s.shape, 0)
            qpos = lax.broadcasted_iota(jnp.int32, s.shape, 1)
            s = jnp.where(kpos <= qpos, s, NEG)
            s_max = jnp.max(s, axis=0, keepdims=True)
        else:
            s_max = smax_refs[par][h][...]
        m = m_refs[h][...]
        m_new = jnp.maximum(m, s_max)
        m_refs[h][...] = m_new
        p = jnp.exp2(s - m_new).astype(BF16)
        pv = jnp.dot(vt_ref[0, h, step], p, preferred_element_type=F32)
        acc_refs[h][...] = jnp.exp2(m - m_new) * acc_refs[h][...] + pv

    for h in range(nh):
        m_refs[h][...] = jnp.full((1, tq), NEG, F32)
        acc_refs[h][...] = jnp.zeros((V_ROWS, tq), F32)
        produce(h, 0, 0)

    def body(t, _):
        for par in range(2):
            @pl.when((t & 1) == par)
            def _():
                for h in range(nh):
                    produce(h, t + 1, 1 - par)
                    consume(h, t, par)
        return 0

    lax.fori_loop(0, per_q * i, body, 0)
    for d in range(per_q):
        for h in range(nh):
            if d + 1 < per_q:
                produce(h, per_q * i + d + 1, (d + 1) & 1)
            consume(h, per_q * i + d, d & 1, diagonal=d)

    for h in range(nh):
        acc = acc_refs[h][...]
        out = acc[:HEAD_DIM] / acc[HEAD_DIM:HEAD_DIM + 1]
        o_ref[0, :, h * HEAD_DIM:(h + 1) * HEAD_DIM] = out.T.astype(BF16)


def _moba_attn_kernel(qt_ref, k_ref, vt_ref, o_ref, kmean_ref, bias_ref, *scratch):
    i = pl.program_id(2)
    nh, _, tq = qt_ref.shape[1:]
    steps = k_ref.shape[2]
    nb = TILE // MOBA_BLOCK
    nblk = steps * nb

    @pl.when(i == 0)
    def _():
        for h in range(nh):
            for n in range(nblk):
                kb = k_ref[0, h, n // nb, (n % nb) * MOBA_BLOCK:(n % nb + 1) * MOBA_BLOCK, :]
                kmean_ref[h, n:n + 1, :] = jnp.mean(kb.astype(F32), axis=0, keepdims=True)

    nidx = lax.broadcasted_iota(jnp.int32, (nblk, tq), 0)
    own = (i * tq + lax.broadcasted_iota(jnp.int32, (nblk, tq), 1)) >> 8
    past = nidx < own
    for h in range(nh):
        qt = qt_ref[0, h]
        km = kmean_ref[h]
        km_hi = km.astype(BF16)
        km_lo = (km - km_hi.astype(F32)).astype(BF16)
        gate = (jnp.dot(km_hi, qt, preferred_element_type=F32)
                + jnp.dot(km_lo, qt, preferred_element_type=F32))
        g1 = jnp.where(past, gate, -jnp.inf)
        thr = jnp.max(g1, axis=0, keepdims=True)
        g = g1
        for _ in range(MOBA_TOPK - 1):
            g = jnp.where(g >= thr, -jnp.inf, g)
            thr = jnp.max(g, axis=0, keepdims=True)
        keep = (past & (g1 >= thr)) | (nidx == own)
        bias = jnp.where(keep, 0.0, NEG)
        for n in range(nblk):
            bias_ref[h, n] = bias[n:n + 1, :]

    _attention(qt_ref, k_ref, vt_ref, o_ref, scratch, i, bias_ref)


def _causal_attn_kernel(qt_ref, k_ref, vt_ref, o_ref, *scratch):
    _attention(qt_ref, k_ref, vt_ref, o_ref, scratch, pl.program_id(2))


def _attn_call(kernel_fn, name, qt, k, vt, scratch):
    batch, heads, dqk, seq = qt.shape
    steps = seq // TILE
    nh = ATTN_HEADS
    return pl.pallas_call(
        kernel_fn,
        grid=(batch, heads // nh, seq // Q_TILE),
        in_specs=[
            pl.BlockSpec((1, nh, dqk, Q_TILE), lambda b, h, i: (b, h, 0, i)),
            pl.BlockSpec((1, nh, steps, TILE, dqk), lambda b, h, i: (b, h, 0, 0, 0)),
            pl.BlockSpec((1, nh, steps, V_ROWS, TILE), lambda b, h, i: (b, h, 0, 0, 0)),
        ],
        out_specs=pl.BlockSpec((1, Q_TILE, nh * HEAD_DIM), lambda b, h, i: (b, i, h)),
        out_shape=jax.ShapeDtypeStruct((batch, seq, heads * HEAD_DIM), BF16),
        scratch_shapes=scratch + _attn_scratch(nh),
        compiler_params=pltpu.CompilerParams(
            dimension_semantics=("arbitrary", "arbitrary", "arbitrary"),
            vmem_limit_bytes=VMEM_LIMIT),
        name=name,
    )(qt, k, vt)


def _moba_attn(qt, k, vt):
    seq = qt.shape[3]
    scratch = [pltpu.VMEM((ATTN_HEADS, seq // MOBA_BLOCK, HEAD_DIM), F32),
               pltpu.VMEM((ATTN_HEADS, seq // MOBA_BLOCK, 1, Q_TILE), F32)]
    return _attn_call(_moba_attn_kernel, "moba_attn", qt, k, vt, scratch)


def _causal_attn(qt, k, vt):
    return _attn_call(_causal_attn_kernel, "mla_attn", qt, k, vt, [])


def _rope_pairs32(x, cos, sin_lo, sin_hi):
    n = x.shape[1]
    return x * cos + pltpu.roll(x, n - 32, 1) * sin_lo + pltpu.roll(x, 32, 1) * sin_hi


def _mla_proj_kernel(x_ref, g_ref, wa_ref, qn_ref, kvn_ref, wuq_ref, wukv_ref,
                     cos_ref, slo_ref, shi_ref, qt_ref, k_ref, vt_ref):
    hn = _rms(x_ref[...], g_ref[...])
    r = jnp.dot(hn.astype(BF16), wa_ref[...], preferred_element_type=F32)
    cq = _rms(r[:, :MLA_Q_RANK], qn_ref[...])
    ckv = _rms(r[:, MLA_Q_RANK:MLA_Q_RANK + MLA_KV_RANK], kvn_ref[...])
    kr = r[:, MLA_Q_RANK + MLA_KV_RANK:]
    q = jnp.dot(cq.astype(BF16), wuq_ref[...], preferred_element_type=F32)
    kv = jnp.dot(ckv.astype(BF16), wukv_ref[...], preferred_element_type=F32)
    cos = cos_ref[...]
    slo = slo_ref[...]
    shi = shi_ref[...]
    nn = HEADS * MLA_NOPE
    reps = HEADS * MLA_ROPE // 128
    qr = _rope_pairs32(q[:, nn:], jnp.concatenate([cos] * reps, axis=1),
                       jnp.concatenate([slo] * reps, axis=1), jnp.concatenate([shi] * reps, axis=1))
    kr = _rope_pairs32(kr, cos, slo, shi)
    scale = (MLA_NOPE + MLA_ROPE) ** -0.5 * LOG2E
    low_half = lax.broadcasted_iota(jnp.int32, kr.shape, 1) < MLA_ROPE
    for h in range(HEADS):
        pair = qr[:, (h // 2) * 128:(h // 2 + 1) * 128]
        if h % 2:
            pair = pltpu.roll(pair, MLA_ROPE, 1)
        qh = jnp.concatenate(
            [q[:, h * MLA_NOPE:(h + 1) * MLA_NOPE], jnp.where(low_half, pair, 0.0)], axis=1)
        qt_ref[0, h] = (qh * scale).T.astype(BF16)
        kh = jnp.concatenate([kv[:, h * MLA_NOPE:(h + 1) * MLA_NOPE], kr], axis=1).astype(BF16)
        vh = kv[:, nn + h * HEAD_DIM:nn + (h + 1) * HEAD_DIM]
        _store_kv(k_ref, vt_ref, h, kh, vh)


def _mla_proj(x, g, wa, qn, kvn, wuq, wukv, cos, slo, shi, batch, seq):
    tiles = seq // TILE
    full = lambda a: pl.BlockSpec(a.shape, lambda b, t: (0,) * a.ndim)
    tab = pl.BlockSpec((TILE, 128), lambda b, t: (t, 0))
    out_specs, out_shape = _qkv_out(batch, seq, MLA_QK_PAD)
    return pl.pallas_call(
        _mla_proj_kernel,
        grid=(batch, tiles),
        in_specs=[
            pl.BlockSpec((TILE, D_MODEL), lambda b, t: (b * tiles + t, 0)),
            full(g), full(wa), full(qn), full(kvn), full(wuq), full(wukv), tab, tab, tab,
        ],
        out_specs=out_specs,
        out_shape=out_shape,
        compiler_params=pltpu.CompilerParams(
            dimension_semantics=("arbitrary", "arbitrary"), vmem_limit_bytes=VMEM_LIMIT),
        name="mla_proj",
    )(x, g, wa, qn, kvn, wuq, wukv, cos, slo, shi)


def _post_attn_kernel(h_ref, o_ref, wo_ref, g_ref, wr_hi_ref, wr_lo_ref, h1_ref, hn_ref, comb_ref,
                      grow_ref):
    h1 = h_ref[...] + jnp.dot(o_ref[...], wo_ref[...], preferred_element_type=F32)
    h1_ref[...] = h1
    hn = _rms(h1, g_ref[...])
    hi = hn.astype(BF16)
    hn_ref[...] = hi
    lo = (hn - hi.astype(F32)).astype(BF16)
    w_hi = wr_hi_ref[...]
    logits = (jnp.dot(hi, w_hi, preferred_element_type=F32)
              + jnp.dot(hi, wr_lo_ref[...], preferred_element_type=F32)
              + jnp.dot(lo, w_hi, preferred_element_type=F32))
    lane_i = lax.broadcasted_iota(jnp.int32, logits.shape, 1)
    lane = lane_i.astype(F32)
    is_group = (lane_i >= N_EXPERTS) & (lane_i < N_EXPERTS + N_GROUPS)

    def first_max(mask):
        v = jnp.max(jnp.where(mask, logits, -jnp.inf), axis=1, keepdims=True)
        idx = jnp.min(jnp.where(mask & (logits == v), lane, 1e9), axis=1, keepdims=True)
        return v, idx

    g_max, g_lane = first_max(is_group)
    g_sum = jnp.sum(jnp.where(is_group, jnp.exp(logits - g_max), 0.0), axis=1, keepdims=True)
    g_w = 1.0 / g_sum
    group_of_lane = (lane_i >> 2).astype(F32)
    in_group = (lane_i < N_EXPERTS) & (group_of_lane == (g_lane - N_EXPERTS))
    v1, i1 = first_max(in_group)
    v2, i2 = first_max(in_group & (lane != i1))
    t = jnp.exp(v2 - v1)
    w1 = g_w / (1.0 + t)
    w2 = w1 * t
    comb_ref[...] = jnp.where(lane == i1, w1, 0.0) + jnp.where(lane == i2, w2, 0.0)
    picked = jnp.where(lane == g_lane, 1.0, 0.0).astype(BF16)
    srow = lax.broadcasted_iota(jnp.int32, (8, ROUTER_LANES), 0)
    slane = lax.broadcasted_iota(jnp.int32, (8, ROUTER_LANES), 1)
    is_g = (srow == 0) & (slane >= N_EXPERTS) & (slane < N_EXPERTS + N_GROUPS)
    sel = jnp.where(is_g, (slane - N_EXPERTS).astype(F32), 0.0).astype(BF16)
    grow_ref[0] = _nt_dot(sel, picked)


def _post_attn(h, o, wo, g, wr_hi, wr_lo):
    n = h.shape[0]
    tile = POST_TILE
    row = lambda w: pl.BlockSpec((tile, w), lambda t: (t, 0))
    full = lambda a: pl.BlockSpec(a.shape, lambda t: (0,) * a.ndim)
    return pl.pallas_call(
        _post_attn_kernel,
        grid=(n // tile,),
        in_specs=[row(D_MODEL), row(D_MODEL), full(wo), full(g), full(wr_hi), full(wr_lo)],
        out_specs=[row(D_MODEL), row(D_MODEL), row(ROUTER_LANES),
                   pl.BlockSpec((1, 8, tile), lambda t: (t, 0, 0))],
        out_shape=[
            jax.ShapeDtypeStruct((n, D_MODEL), F32),
            jax.ShapeDtypeStruct((n, D_MODEL), BF16),
            jax.ShapeDtypeStruct((n, ROUTER_LANES), F32),
            jax.ShapeDtypeStruct((n // tile, 8, tile), F32),
        ],
        compiler_params=pltpu.CompilerParams(
            dimension_semantics=("arbitrary",), vmem_limit_bytes=VMEM_LIMIT),
        name="post_attn",
    )(h, o, wo, g, wr_hi, wr_lo)


def _moe_ple_kernel(hn_ref, comb_ref, grow_ref, h1_ref, p_ref, tri_ref, wg_ref, wu_ref, wd_ref,
                    pn_ref, pg_ref, pp_ref, fn_ref, out_ref, xs_ref, cs_ref, ys_ref, *, final):
    t = hn_ref.shape[0]
    gid = grow_ref[0, 0:1, :]
    g_iota = lax.broadcasted_iota(jnp.int32, (8, t), 0).astype(F32)
    member = gid == g_iota
    onehot = jnp.where(member, 1.0, 0.0)
    rank = jnp.dot(onehot.astype(BF16), tri_ref[...], preferred_element_type=F32)
    starts, rows = [], []
    start = jnp.int32(0)
    for g in range(N_GROUPS):
        count = jnp.sum(jnp.where(gid == float(g), 1, 0).astype(jnp.int32))
        padded = ((count + (SORT_ALIGN - 1)) >> 4) << 4
        starts.append(start)
        rows.append(padded)
        start = start + padded
    start_vec = jnp.zeros((8, t), F32)
    for g in range(1, N_GROUPS):
        start_vec = jnp.where(g_iota == float(g), starts[g].astype(F32), start_vec)
    dest = jnp.sum(jnp.where(member, rank + start_vec, 0.0), axis=0, keepdims=True)
    r_iota = lax.broadcasted_iota(jnp.int32, (SORT_ROWS, t), 0).astype(F32)
    perm = jnp.where(r_iota == dest, 1.0, 0.0).astype(BF16)

    comb = comb_ref[...]
    comb_hi = comb.astype(BF16)
    comb_lo = (comb - comb_hi.astype(F32)).astype(BF16)
    xs_ref[0:SORT_ROWS, :] = jnp.dot(perm, hn_ref[...], preferred_element_type=F32).astype(BF16)
    cs_ref[0:SORT_ROWS, :] = (jnp.dot(perm, comb_hi, preferred_element_type=F32)
                              + jnp.dot(perm, comb_lo, preferred_element_type=F32))
    xs_ref[SORT_ROWS:, :] = jnp.zeros((MOE_CHUNK, D_MODEL), BF16)
    cs_ref[SORT_ROWS:, :] = jnp.zeros((MOE_CHUNK, ROUTER_LANES), F32)
    ys_ref[...] = jnp.zeros(ys_ref.shape, BF16)

    lane = lax.broadcasted_iota(jnp.int32, (MOE_CHUNK, ROUTER_LANES), 1)
    for g in range(N_GROUPS):
        def chunk(j, _, g=g):
            off = pl.multiple_of(starts[g] + j * MOE_CHUNK, SORT_ALIGN)
            xc = xs_ref[pl.ds(off, MOE_CHUNK), :]
            a = jnp.dot(xc, wg_ref[g], preferred_element_type=F32)
            u = jnp.dot(xc, wu_ref[g], preferred_element_type=F32)
            act = a * jax.nn.sigmoid(a) * u
            cc = cs_ref[pl.ds(off, MOE_CHUNK), :]
            parts = []
            for e in range(PER_GROUP):
                c = jnp.sum(jnp.where(lane == g * PER_GROUP + e, cc, 0.0), axis=1, keepdims=True)
                parts.append((act[:, e * EXPERT_FF:(e + 1) * EXPERT_FF] * c).astype(BF16))
            y = jnp.dot(jnp.concatenate(parts, axis=1), wd_ref[g], preferred_element_type=F32)
            ys_ref[pl.ds(off, MOE_CHUNK), :] = y.astype(BF16)
            return 0

        n_chunks = sum((rows[g] > k * MOE_CHUNK).astype(jnp.int32)
                       for k in range(SORT_ROWS // MOE_CHUNK))
        lax.fori_loop(0, n_chunks, chunk, 0)

    moe = lax.dot_general(perm, ys_ref[0:SORT_ROWS, :], (((0,), (0,)), ((), ())),
                          preferred_element_type=F32)
    h2 = h1_ref[...] + moe
    gate = jax.nn.sigmoid(
        jnp.dot(_rms(h2, pn_ref[...]).astype(BF16), pg_ref[...], preferred_element_type=F32))
    proj = jnp.dot(p_ref[...].astype(BF16), pp_ref[...], preferred_element_type=F32)
    h3 = h2 + gate * proj
    out_ref[...] = _rms(h3, fn_ref[...]) if final else h3


def _moe_ple(hn, comb, grow, h1, p, wg, wu, wd, pn, pg, pp, fn, final):
    n = hn.shape[0]
    tile = POST_TILE
    row = lambda w: pl.BlockSpec((tile, w), lambda t: (t, 0))
    once = lambda a: pl.BlockSpec(a.shape, lambda t: (0,) * a.ndim, pipeline_mode=pl.Buffered(1))
    tri = (jnp.arange(tile)[:, None] < jnp.arange(tile)[None, :]).astype(BF16)
    return pl.pallas_call(
        functools.partial(_moe_ple_kernel, final=final),
        grid=(n // tile,),
        in_specs=[row(D_MODEL), row(ROUTER_LANES), pl.BlockSpec((1, 8, tile), lambda t: (t, 0, 0)),
                  row(D_MODEL), row(PLE_DIM), once(tri), once(wg), once(wu), once(wd),
                  once(pn), once(pg), once(pp), once(fn)],
        out_specs=row(D_MODEL),
        out_shape=jax.ShapeDtypeStruct((n, D_MODEL), F32),
        scratch_shapes=[pltpu.VMEM((SORT_ROWS + MOE_CHUNK, D_MODEL), BF16),
                        pltpu.VMEM((SORT_ROWS + MOE_CHUNK, ROUTER_LANES), F32),
                        pltpu.VMEM((SORT_ROWS + MOE_CHUNK, D_MODEL), BF16)],
        compiler_params=pltpu.CompilerParams(
            dimension_semantics=("arbitrary",), vmem_limit_bytes=VMEM_LIMIT),
        name="moe_ple",
    )(hn, comb, grow, h1, p, tri, wg, wu, wd, pn, pg, pp, fn)


def _rope_tables(seq, half):
    inv = THETA ** (-jnp.arange(half, dtype=F32) / half)
    ang = jnp.arange(seq, dtype=F32)[:, None] * inv[None, :]
    return jnp.cos(ang), jnp.sin(ang)


def _split_bf16(w):
    hi = w.astype(BF16)
    return hi, (w - hi.astype(F32)).astype(BF16)


def kernel(x, p, attn_norm, ffn_norm, ple_norm, final_norm, moba_wqkv, moba_wo, mla_wdq, mla_qnorm,
           mla_wuq, mla_wdkv, mla_kvnorm, mla_wukv, mla_wo, moe_wgroup, moe_wexpert, moe_wgate,
           moe_wup, moe_wdown, ple_gate, ple_proj):
    batch, seq, d = x.shape
    n = batch * seq
    h = x.reshape(n, d)

    cos64, sin64 = _rope_tables(seq, HEAD_DIM // 2)
    moba_cos = jnp.concatenate([cos64, cos64], axis=1)
    moba_sin = jnp.concatenate([-sin64, sin64], axis=1)
    cos32, sin32 = _rope_tables(seq, MLA_ROPE // 2)
    z32 = jnp.zeros_like(sin32)
    mla_cos = jnp.concatenate([cos32] * 4, axis=1)
    mla_slo = jnp.concatenate([-sin32, z32, -sin32, z32], axis=1)
    mla_shi = jnp.concatenate([z32, sin32, z32, sin32], axis=1)

    row = lambda v: v.reshape(1, -1)

    for layer in range(N_LAYERS):
        j = layer // 2
        if layer % 2 == 0:
            q, k, vt = _moba_proj(h, row(attn_norm[layer]), moba_wqkv[j].astype(BF16),
                                  moba_cos, moba_sin, batch, seq)
            o = _moba_attn(q, k, vt)
            wo = moba_wo[j]
        else:
            wuq = mla_wuq[j].reshape(MLA_Q_RANK, HEADS, MLA_NOPE + MLA_ROPE)
            wuq = jnp.concatenate([wuq[:, :, :MLA_NOPE].reshape(MLA_Q_RANK, -1),
                                   wuq[:, :, MLA_NOPE:].reshape(MLA_Q_RANK, -1)], axis=1)
            wukv = mla_wukv[j].reshape(MLA_KV_RANK, HEADS, MLA_NOPE + HEAD_DIM)
            wukv = jnp.concatenate([wukv[:, :, :MLA_NOPE].reshape(MLA_KV_RANK, -1),
                                    wukv[:, :, MLA_NOPE:].reshape(MLA_KV_RANK, -1)], axis=1)
            wa = jnp.concatenate(
                [mla_wdq[j], mla_wdkv[j], jnp.zeros((d, 128 - MLA_ROPE), F32)], axis=1)
            q, k, vt = _mla_proj(h, row(attn_norm[layer]), wa.astype(BF16), row(mla_qnorm[j]),
                                 row(mla_kvnorm[j]), wuq.astype(BF16), wukv.astype(BF16),
                                 mla_cos, mla_slo, mla_shi, batch, seq)
            o = _causal_attn(q, k, vt)
            wo = mla_wo[j]

        wr = jnp.concatenate(
            [moe_wexpert[layer], moe_wgroup[layer],
             jnp.zeros((d, ROUTER_LANES - N_EXPERTS - N_GROUPS), F32)], axis=1)
        wr_hi, wr_lo = _split_bf16(wr)
        h1, hn, comb, grow = _post_attn(h, o.reshape(n, -1), wo.astype(BF16), row(ffn_norm[layer]),
                                        wr_hi, wr_lo)

        def by_group(w):
            w = w.reshape(N_GROUPS, PER_GROUP, d, EXPERT_FF).transpose(0, 2, 1, 3)
            return w.reshape(N_GROUPS, d, PER_GROUP * EXPERT_FF).astype(BF16)

        wd = moe_wdown[layer].reshape(N_GROUPS, PER_GROUP * EXPERT_FF, d).astype(BF16)
        h = _moe_ple(hn, comb, grow, h1, p[layer].reshape(n, PLE_DIM), by_group(moe_wgate[layer]),
                     by_group(moe_wup[layer]), wd, row(ple_norm[layer]), ple_gate[layer].astype(BF16),
                     ple_proj[layer].astype(BF16), row(final_norm), layer == N_LAYERS - 1)

    return h.reshape(batch, seq, d)
```

```python
import functools

import jax
import jax.numpy as jnp
from jax import lax
from jax.experimental import pallas as pl
from jax.experimental.pallas import tpu as pltpu

F32 = jnp.float32
BF16 = jnp.bfloat16

D_MODEL = 1024
N_LAYERS = 2
EPS = 1e-6
THETA = 10000.0
NEG = -1e30
LOG2E = 1.4426950408889634

HEADS = 8
HEAD_DIM = 128
MOBA_BLOCK = 256
MOBA_TOPK = 3

MLA_Q_RANK = 384
MLA_KV_RANK = 256
MLA_NOPE = 128
MLA_ROPE = 64
MLA_QK_PAD = 256

N_GROUPS = 4
PER_GROUP = 4
N_EXPERTS = 16
EXPERT_FF = 256
ROUTER_LANES = 128

PLE_DIM = 256

TILE = 512
Q_TILE = 1024
V_ROWS = HEAD_DIM + 16
ATTN_HEADS = 2
POST_TILE = 512
SORT_ALIGN = 16
SORT_ROWS = 640
MOE_CHUNK = 160

VMEM_LIMIT = 56 * 1024 * 1024


def _rms(x, g):
    return x * lax.rsqrt(jnp.mean(x * x, axis=-1, keepdims=True) + EPS) * g


def _nt_dot(a, b):
    return lax.dot_general(a, b, (((1,), (1,)), ((), ())), preferred_element_type=F32)


def _moba_proj_kernel(x_ref, g_ref, w_ref, cos_ref, sin_ref, qt_ref, k_ref, vt_ref):
    hn = _rms(x_ref[...], g_ref[...])
    qkv = jnp.dot(hn.astype(BF16), w_ref[...], preferred_element_type=F32)
    cos = cos_ref[...]
    sin = sin_ref[...]
    scale = HEAD_DIM ** -0.5 * LOG2E
    hd = HEADS * HEAD_DIM
    for h in range(HEADS):
        lo = h * HEAD_DIM
        qh = qkv[:, lo:lo + HEAD_DIM]
        qh = qh * cos + pltpu.roll(qh, HEAD_DIM // 2, 1) * sin
        qt_ref[0, h] = (qh * scale).T.astype(BF16)
        kh = qkv[:, hd + lo:hd + lo + HEAD_DIM]
        kh = (kh * cos + pltpu.roll(kh, HEAD_DIM // 2, 1) * sin).astype(BF16)
        vh = qkv[:, 2 * hd + lo:2 * hd + lo + HEAD_DIM]
        _store_kv(k_ref, vt_ref, h, kh, vh)


def _store_kv(k_ref, vt_ref, h, kh, vh):
    k_ref[0, h, 0] = kh
    vt_ref[0, h, 0] = jnp.concatenate(
        [vh.T, jnp.ones((V_ROWS - HEAD_DIM, TILE), F32)], axis=0).astype(BF16)


def _qkv_out(batch, seq, dqk):
    steps = seq // TILE
    specs = [
        pl.BlockSpec((1, HEADS, dqk, TILE), lambda b, t: (b, 0, 0, t)),
        pl.BlockSpec((1, HEADS, 1, TILE, dqk), lambda b, t: (b, 0, t, 0, 0)),
        pl.BlockSpec((1, HEADS, 1, V_ROWS, TILE), lambda b, t: (b, 0, t, 0, 0)),
    ]
    shapes = [
        jax.ShapeDtypeStruct((batch, HEADS, dqk, seq), BF16),
        jax.ShapeDtypeStruct((batch, HEADS, steps, TILE, dqk), BF16),
        jax.ShapeDtypeStruct((batch, HEADS, steps, V_ROWS, TILE), BF16),
    ]
    return specs, shapes


def _moba_proj(x, g, w, cos2, sin2, batch, seq):
    tiles = seq // TILE
    out_specs, out_shape = _qkv_out(batch, seq, HEAD_DIM)
    return pl.pallas_call(
        _moba_proj_kernel,
        grid=(batch, tiles),
        in_specs=[
            pl.BlockSpec((TILE, D_MODEL), lambda b, t: (b * tiles + t, 0)),
            pl.BlockSpec((1, D_MODEL), lambda b, t: (0, 0)),
            pl.BlockSpec((D_MODEL, 3 * HEADS * HEAD_DIM), lambda b, t: (0, 0)),
            pl.BlockSpec((TILE, HEAD_DIM), lambda b, t: (t, 0)),
            pl.BlockSpec((TILE, HEAD_DIM), lambda b, t: (t, 0)),
        ],
        out_specs=out_specs,
        out_shape=out_shape,
        compiler_params=pltpu.CompilerParams(
            dimension_semantics=("arbitrary", "arbitrary"), vmem_limit_bytes=VMEM_LIMIT),
        name="moba_proj",
    )(x, g, w, cos2, sin2)


def _attn_scratch(nh):
    return ([pltpu.VMEM((TILE, Q_TILE), F32)] * (2 * nh) + [pltpu.VMEM((1, Q_TILE), F32)] * (2 * nh)
            + [pltpu.VMEM((1, Q_TILE), F32)] * nh + [pltpu.VMEM((V_ROWS, Q_TILE), F32)] * nh)


def _attention(qt_ref, k_ref, vt_ref, o_ref, scratch, i, bias_ref=None):
    nh, _, tq = qt_ref.shape[1:]
    assert tq == 2 * TILE
    s_refs = (scratch[:nh], scratch[nh:2 * nh])
    smax_refs = (scratch[2 * nh:3 * nh], scratch[3 * nh:4 * nh])
    m_refs = scratch[4 * nh:5 * nh]
    acc_refs = scratch[5 * nh:6 * nh]
    everything, left, right = slice(0, tq), slice(0, TILE), slice(TILE, tq)

    def produce(h, step, par, cols=everything):
        s = jnp.dot(k_ref[0, h, step], qt_ref[0, h, :, cols], preferred_element_type=F32)
        if bias_ref is not None:
            nb = TILE // MOBA_BLOCK
            b = bias_ref[h, pl.ds(step * nb, nb), :, cols]
            s = (s.reshape(nb, MOBA_BLOCK, -1) + b).reshape(s.shape)
        s_refs[par][h][:, cols] = s
        smax_refs[par][h][:, cols] = jnp.max(s, axis=0, keepdims=True)

    def consume(h, step, par, cols=everything, causal=False):
        s = s_refs[par][h][:, cols]
        if causal:
            kpos = lax.broadcasted_iota(jnp.int32, s.shape, 0)
            qpos = lax.broadcasted_iota(jnp.int32, s.shape, 1)
            s = jnp.where(kpos <= qpos, s, NEG)
            s_max = jnp.max(s, axis=0, keepdims=True)
        else:
            s_max = smax_refs[par][h][:, cols]
        m = m_refs[h][:, cols]
        m_new = jnp.maximum(m, s_max)
        m_refs[h][:, cols] = m_new
        p = jnp.exp2(s - m_new).astype(BF16)
        pv = jnp.dot(vt_ref[0, h, step], p, preferred_element_type=F32)
        acc_refs[h][:, cols] = jnp.exp2(m - m_new) * acc_refs[h][:, cols] + pv

    for h in range(nh):
        m_refs[h][...] = jnp.full((1, tq), NEG, F32)
        acc_refs[h][...] = jnp.zeros((V_ROWS, tq), F32)
        produce(h, 0, 0)

    def body(t, _):
        for par in range(2):
            @pl.when((t & 1) == par)
            def _():
                for h in range(nh):
                    produce(h, t + 1, 1 - par)
                    consume(h, t, par)
        return 0

    lax.fori_loop(0, 2 * i, body, 0)
    for h in range(nh):
        produce(h, 2 * i + 1, 1, right)
        consume(h, 2 * i, 0, left, causal=True)
        consume(h, 2 * i, 0, right)
    for h in range(nh):
        consume(h, 2 * i + 1, 1, right, causal=True)

    for h in range(nh):
        acc = acc_refs[h][...]
        out = acc[:HEAD_DIM] / acc[HEAD_DIM:HEAD_DIM + 1]
        o_ref[0, :, h * HEAD_DIM:(h + 1) * HEAD_DIM] = out.T.astype(BF16)


def _moba_attn_kernel(qt_ref, k_ref, vt_ref, o_ref, kmean_ref, bias_ref, *scratch):
    i = pl.program_id(2)
    nh, _, tq = qt_ref.shape[1:]
    steps = k_ref.shape[2]
    nb = TILE // MOBA_BLOCK
    nblk = steps * nb

    @pl.when(i == 0)
    def _():
        for h in range(nh):
            for n in range(nblk):
                kb = k_ref[0, h, n // nb, (n % nb) * MOBA_BLOCK:(n % nb + 1) * MOBA_BLOCK, :]
                kmean_ref[h, n:n + 1, :] = jnp.mean(kb.astype(F32), axis=0, keepdims=True)

    nidx = lax.broadcasted_iota(jnp.int32, (nblk, tq), 0)
    own = (i * tq + lax.broadcasted_iota(jnp.int32, (nblk, tq), 1)) >> 8
    past = nidx < own
    for h in range(nh):
        qt = qt_ref[0, h]
        km = kmean_ref[h]
        km_hi = km.astype(BF16)
        km_lo = (km - km_hi.astype(F32)).astype(BF16)
        gate = (jnp.dot(km_hi, qt, preferred_element_type=F32)
                + jnp.dot(km_lo, qt, preferred_element_type=F32))
        g1 = jnp.where(past, gate, -jnp.inf)
        thr = jnp.max(g1, axis=0, keepdims=True)
        g = g1
        for _ in range(MOBA_TOPK - 1):
            g = jnp.where(g >= thr, -jnp.inf, g)
            thr = jnp.max(g, axis=0, keepdims=True)
        keep = (past & (g1 >= thr)) | (nidx == own)
        bias = jnp.where(keep, 0.0, NEG)
        for n in range(nblk):
            bias_ref[h, n] = bias[n:n + 1, :]

    _attention(qt_ref, k_ref, vt_ref, o_ref, scratch, i, bias_ref)


def _causal_attn_kernel(qt_ref, k_ref, vt_ref, o_ref, *scratch):
    _attention(qt_ref, k_ref, vt_ref, o_ref, scratch, pl.program_id(2))


def _attn_call(kernel_fn, name, qt, k, vt, scratch):
    batch, heads, dqk, seq = qt.shape
    steps = seq // TILE
    nh = ATTN_HEADS
    return pl.pallas_call(
        kernel_fn,
        grid=(batch, heads // nh, seq // Q_TILE),
        in_specs=[
            pl.BlockSpec((1, nh, dqk, Q_TILE), lambda b, h, i: (b, h, 0, i)),
            pl.BlockSpec((1, nh, steps, TILE, dqk), lambda b, h, i: (b, h, 0, 0, 0)),
            pl.BlockSpec((1, nh, steps, V_ROWS, TILE), lambda b, h, i: (b, h, 0, 0, 0)),
        ],
        out_specs=pl.BlockSpec((1, Q_TILE, nh * HEAD_DIM), lambda b, h, i: (b, i, h)),
        out_shape=jax.ShapeDtypeStruct((batch, seq, heads * HEAD_DIM), BF16),
        scratch_shapes=scratch + _attn_scratch(nh),
        compiler_params=pltpu.CompilerParams(
            dimension_semantics=("arbitrary", "arbitrary", "arbitrary"),
            vmem_limit_bytes=VMEM_LIMIT),
        name=name,
    )(qt, k, vt)


def _moba_attn(qt, k, vt):
    seq = qt.shape[3]
    scratch = [pltpu.VMEM((ATTN_HEADS, seq // MOBA_BLOCK, HEAD_DIM), F32),
               pltpu.VMEM((ATTN_HEADS, seq // MOBA_BLOCK, 1, Q_TILE), F32)]
    return _attn_call(_moba_attn_kernel, "moba_attn", qt, k, vt, scratch)


def _causal_attn(qt, k, vt):
    return _attn_call(_causal_attn_kernel, "mla_attn", qt, k, vt, [])


def _rope_pairs32(x, cos, sin_lo, sin_hi):
    n = x.shape[1]
    return x * cos + pltpu.roll(x, n - 32, 1) * sin_lo + pltpu.roll(x, 32, 1) * sin_hi


def _mla_proj_kernel(x_ref, g_ref, wa_ref, qn_ref, kvn_ref, wuq_ref, wukv_ref,
                     cos_ref, slo_ref, shi_ref, qt_ref, k_ref, vt_ref):
    hn = _rms(x_ref[...], g_ref[...])
    r = jnp.dot(hn.astype(BF16), wa_ref[...], preferred_element_type=F32)
    cq = _rms(r[:, :MLA_Q_RANK], qn_ref[...])
    ckv = _rms(r[:, MLA_Q_RANK:MLA_Q_RANK + MLA_KV_RANK], kvn_ref[...])
    kr = r[:, MLA_Q_RANK + MLA_KV_RANK:]
    q = jnp.dot(cq.astype(BF16), wuq_ref[...], preferred_element_type=F32)
    kv = jnp.dot(ckv.astype(BF16), wukv_ref[...], preferred_element_type=F32)
    cos = cos_ref[...]
    slo = slo_ref[...]
    shi = shi_ref[...]
    nn = HEADS * MLA_NOPE
    reps = HEADS * MLA_ROPE // 128
    qr = _rope_pairs32(q[:, nn:], jnp.concatenate([cos] * reps, axis=1),
                       jnp.concatenate([slo] * reps, axis=1), jnp.concatenate([shi] * reps, axis=1))
    kr = _rope_pairs32(kr, cos, slo, shi)
    scale = (MLA_NOPE + MLA_ROPE) ** -0.5 * LOG2E
    low_half = lax.broadcasted_iota(jnp.int32, kr.shape, 1) < MLA_ROPE
    for h in range(HEADS):
        pair = qr[:, (h // 2) * 128:(h // 2 + 1) * 128]
        if h % 2:
            pair = pltpu.roll(pair, MLA_ROPE, 1)
        qh = jnp.concatenate(
            [q[:, h * MLA_NOPE:(h + 1) * MLA_NOPE], jnp.where(low_half, pair, 0.0)], axis=1)
        qt_ref[0, h] = (qh * scale).T.astype(BF16)
        kh = jnp.concatenate([kv[:, h * MLA_NOPE:(h + 1) * MLA_NOPE], kr], axis=1).astype(BF16)
        vh = kv[:, nn + h * HEAD_DIM:nn + (h + 1) * HEAD_DIM]
        _store_kv(k_ref, vt_ref, h, kh, vh)


def _mla_proj(x, g, wa, qn, kvn, wuq, wukv, cos, slo, shi, batch, seq):
    tiles = seq // TILE
    full = lambda a: pl.BlockSpec(a.shape, lambda b, t: (0,) * a.ndim)
    tab = pl.BlockSpec((TILE, 128), lambda b, t: (t, 0))
    out_specs, out_shape = _qkv_out(batch, seq, MLA_QK_PAD)
    return pl.pallas_call(
        _mla_proj_kernel,
        grid=(batch, tiles),
        in_specs=[
            pl.BlockSpec((TILE, D_MODEL), lambda b, t: (b * tiles + t, 0)),
            full(g), full(wa), full(qn), full(kvn), full(wuq), full(wukv), tab, tab, tab,
        ],
        out_specs=out_specs,
        out_shape=out_shape,
        compiler_params=pltpu.CompilerParams(
            dimension_semantics=("arbitrary", "arbitrary"), vmem_limit_bytes=VMEM_LIMIT),
        name="mla_proj",
    )(x, g, wa, qn, kvn, wuq, wukv, cos, slo, shi)


def _post_attn_kernel(h_ref, o_ref, wo_ref, g_ref, wr_ref, h1_ref, hn_ref, comb_ref, grow_ref):
    h1 = h_ref[...] + jnp.dot(o_ref[...], wo_ref[...], preferred_element_type=F32)
    h1_ref[...] = h1
    hn = _rms(h1, g_ref[...])
    hi = hn.astype(BF16)
    hn_ref[...] = hi
    lo = (hn - hi.astype(F32)).astype(BF16)
    w2 = wr_ref[...]
    r_hi = jnp.dot(hi, w2, preferred_element_type=F32)
    r_lo = jnp.dot(lo, w2, preferred_element_type=F32)
    logits = ((r_hi[:, :ROUTER_LANES] + r_hi[:, ROUTER_LANES:])
              + (r_lo[:, :ROUTER_LANES] + r_lo[:, ROUTER_LANES:]))
    lane_i = lax.broadcasted_iota(jnp.int32, logits.shape, 1)
    lane = lane_i.astype(F32)
    is_group = (lane_i >= N_EXPERTS) & (lane_i < N_EXPERTS + N_GROUPS)

    def first_max(mask):
        v = jnp.max(jnp.where(mask, logits, -jnp.inf), axis=1, keepdims=True)
        idx = jnp.min(jnp.where(mask & (logits == v), lane, 1e9), axis=1, keepdims=True)
        return v, idx

    g_max, g_lane = first_max(is_group)
    g_sum = jnp.sum(jnp.where(is_group, jnp.exp(logits - g_max), 0.0), axis=1, keepdims=True)
    g_w = 1.0 / g_sum
    group_of_lane = (lane_i >> 2).astype(F32)
    in_group = (lane_i < N_EXPERTS) & (group_of_lane == (g_lane - N_EXPERTS))
    v1, i1 = first_max(in_group)
    v2, i2 = first_max(in_group & (lane != i1))
    t = jnp.exp(v2 - v1)
    w1 = g_w / (1.0 + t)
    w2 = w1 * t
    comb_ref[...] = jnp.where(lane == i1, w1, 0.0) + jnp.where(lane == i2, w2, 0.0)
    picked = jnp.where(lane == g_lane, 1.0, 0.0).astype(BF16)
    srow = lax.broadcasted_iota(jnp.int32, (8, ROUTER_LANES), 0)
    slane = lax.broadcasted_iota(jnp.int32, (8, ROUTER_LANES), 1)
    is_g = (srow == 0) & (slane >= N_EXPERTS) & (slane < N_EXPERTS + N_GROUPS)
    sel = jnp.where(is_g, (slane - N_EXPERTS).astype(F32), 0.0).astype(BF16)
    grow_ref[0] = _nt_dot(sel, picked)


def _post_attn(h, o, wo, g, wr):
    n = h.shape[0]
    tile = POST_TILE
    row = lambda w: pl.BlockSpec((tile, w), lambda t: (t, 0))
    full = lambda a: pl.BlockSpec(a.shape, lambda t: (0,) * a.ndim)
    return pl.pallas_call(
        _post_attn_kernel,
        grid=(n // tile,),
        in_specs=[row(D_MODEL), row(D_MODEL), full(wo), full(g), full(wr)],
        out_specs=[row(D_MODEL), row(D_MODEL), row(ROUTER_LANES),
                   pl.BlockSpec((1, 8, tile), lambda t: (t, 0, 0))],
        out_shape=[
            jax.ShapeDtypeStruct((n, D_MODEL), F32),
            jax.ShapeDtypeStruct((n, D_MODEL), BF16),
            jax.ShapeDtypeStruct((n, ROUTER_LANES), F32),
            jax.ShapeDtypeStruct((n // tile, 8, tile), F32),
        ],
        compiler_params=pltpu.CompilerParams(
            dimension_semantics=("arbitrary",), vmem_limit_bytes=VMEM_LIMIT),
        name="post_attn",
    )(h, o, wo, g, wr)


def _moe_ple_kernel(hn_ref, comb_ref, grow_ref, h1_ref, p_ref, tri_ref, wg_ref, wu_ref, wd_ref,
                    pn_ref, pg_ref, pp_ref, fn_ref, out_ref, xs_ref, cs_ref, ys_ref, *, final):
    t = hn_ref.shape[0]
    gid = grow_ref[0, 0:1, :]
    g_iota = lax.broadcasted_iota(jnp.int32, (8, t), 0).astype(F32)
    member = gid == g_iota
    onehot = jnp.where(member, 1.0, 0.0)
    rank = jnp.dot(onehot.astype(BF16), tri_ref[...], preferred_element_type=F32)
    starts, rows = [], []
    start = jnp.int32(0)
    for g in range(N_GROUPS):
        count = jnp.sum(jnp.where(gid == float(g), 1, 0).astype(jnp.int32))
        padded = ((count + (SORT_ALIGN - 1)) >> 4) << 4
        starts.append(start)
        rows.append(padded)
        start = start + padded
    start_vec = jnp.zeros((8, t), F32)
    for g in range(1, N_GROUPS):
        start_vec = jnp.where(g_iota == float(g), starts[g].astype(F32), start_vec)
    dest = jnp.sum(jnp.where(member, rank + start_vec, 0.0), axis=0, keepdims=True)
    r_iota = lax.broadcasted_iota(jnp.int32, (SORT_ROWS, t), 0).astype(F32)
    perm = jnp.where(r_iota == dest, 1.0, 0.0).astype(BF16)

    comb = comb_ref[...]
    comb_hi = comb.astype(BF16).astype(F32)
    comb_lo = (comb - comb_hi).astype(BF16).astype(F32)
    packed = (comb_hi + pltpu.roll(comb_lo, N_EXPERTS, 1)).astype(BF16)
    both = jnp.dot(perm, jnp.concatenate([hn_ref[...], packed], axis=1),
                   preferred_element_type=F32)
    xs_ref[0:SORT_ROWS, :] = both[:, :D_MODEL].astype(BF16)
    cs_ref[0:SORT_ROWS, :] = both[:, D_MODEL:]
    xs_ref[SORT_ROWS:, :] = jnp.zeros((MOE_CHUNK, D_MODEL), BF16)
    cs_ref[SORT_ROWS:, :] = jnp.zeros((MOE_CHUNK, ROUTER_LANES), F32)
    ys_ref[...] = jnp.zeros(ys_ref.shape, BF16)

    lane = lax.broadcasted_iota(jnp.int32, (MOE_CHUNK, ROUTER_LANES), 1)
    for g in range(N_GROUPS):
        def chunk(j, _, g=g):
            off = pl.multiple_of(starts[g] + j * MOE_CHUNK, SORT_ALIGN)
            xc = xs_ref[pl.ds(off, MOE_CHUNK), :]
            cc = cs_ref[pl.ds(off, MOE_CHUNK), :]
            parts = []
            for e in range(g * PER_GROUP, (g + 1) * PER_GROUP):
                a = jnp.dot(xc, wg_ref[e], preferred_element_type=F32)
                u = jnp.dot(xc, wu_ref[e], preferred_element_type=F32)
                c = jnp.sum(jnp.where((lane == e) | (lane == e + N_EXPERTS), cc, 0.0),
                            axis=1, keepdims=True)
                parts.append((a * jax.nn.sigmoid(a) * u * c).astype(BF16))
            y = jnp.dot(jnp.concatenate(parts, axis=1), wd_ref[g], preferred_element_type=F32)
            ys_ref[pl.ds(off, MOE_CHUNK), :] = y.astype(BF16)
            return 0

        n_chunks = sum((rows[g] > k * MOE_CHUNK).astype(jnp.int32)
                       for k in range(SORT_ROWS // MOE_CHUNK))
        lax.fori_loop(0, n_chunks, chunk, 0)

    moe = lax.dot_general(perm, ys_ref[0:SORT_ROWS, :], (((0,), (0,)), ((), ())),
                          preferred_element_type=F32)
    h2 = h1_ref[...] + moe
    gate = jax.nn.sigmoid(
        jnp.dot(_rms(h2, pn_ref[...]).astype(BF16), pg_ref[...], preferred_element_type=F32))
    proj = jnp.dot(p_ref[...].astype(BF16), pp_ref[...], preferred_element_type=F32)
    h3 = h2 + gate * proj
    out_ref[...] = _rms(h3, fn_ref[...]) if final else h3


def _moe_ple(hn, comb, grow, h1, p, wg, wu, wd, pn, pg, pp, fn, final):
    n = hn.shape[0]
    tile = POST_TILE
    row = lambda w: pl.BlockSpec((tile, w), lambda t: (t, 0))
    once = lambda a: pl.BlockSpec(a.shape, lambda t: (0,) * a.ndim, pipeline_mode=pl.Buffered(1))
    tri = (jnp.arange(tile)[:, None] < jnp.arange(tile)[None, :]).astype(BF16)
    return pl.pallas_call(
        functools.partial(_moe_ple_kernel, final=final),
        grid=(n // tile,),
        in_specs=[row(D_MODEL), row(ROUTER_LANES), pl.BlockSpec((1, 8, tile), lambda t: (t, 0, 0)),
                  row(D_MODEL), row(PLE_DIM), once(tri), once(wg), once(wu), once(wd),
                  once(pn), once(pg), once(pp), once(fn)],
        out_specs=row(D_MODEL),
        out_shape=jax.ShapeDtypeStruct((n, D_MODEL), F32),
        scratch_shapes=[pltpu.VMEM((SORT_ROWS + MOE_CHUNK, D_MODEL), BF16),
                        pltpu.VMEM((SORT_ROWS + MOE_CHUNK, ROUTER_LANES), F32),
                        pltpu.VMEM((SORT_ROWS + MOE_CHUNK, D_MODEL), BF16)],
        compiler_params=pltpu.CompilerParams(
            dimension_semantics=("arbitrary",), vmem_limit_bytes=VMEM_LIMIT),
        name="moe_ple",
    )(hn, comb, grow, h1, p, tri, wg, wu, wd, pn, pg, pp, fn)


def _rope_tables(seq, half):
    inv = THETA ** (-jnp.arange(half, dtype=F32) / half)
    ang = jnp.arange(seq, dtype=F32)[:, None] * inv[None, :]
    return jnp.cos(ang), jnp.sin(ang)


def _split_bf16(w):
    hi = w.astype(BF16)
    return hi, (w - hi.astype(F32)).astype(BF16)


def kernel(x, p, attn_norm, ffn_norm, ple_norm, final_norm, moba_wqkv, moba_wo, mla_wdq, mla_qnorm,
           mla_wuq, mla_wdkv, mla_kvnorm, mla_wukv, mla_wo, moe_wgroup, moe_wexpert, moe_wgate,
           moe_wup, moe_wdown, ple_gate, ple_proj):
    batch, seq, d = x.shape
    n = batch * seq
    h = x.reshape(n, d)

    cos64, sin64 = _rope_tables(seq, HEAD_DIM // 2)
    moba_cos = jnp.concatenate([cos64, cos64], axis=1)
    moba_sin = jnp.concatenate([-sin64, sin64], axis=1)
    cos32, sin32 = _rope_tables(seq, MLA_ROPE // 2)
    z32 = jnp.zeros_like(sin32)
    mla_cos = jnp.concatenate([cos32] * 4, axis=1)
    mla_slo = jnp.concatenate([-sin32, z32, -sin32, z32], axis=1)
    mla_shi = jnp.concatenate([z32, sin32, z32, sin32], axis=1)

    row = lambda v: v.reshape(1, -1)

    for layer in range(N_LAYERS):
        j = layer // 2
        if layer % 2 == 0:
            q, k, vt = _moba_proj(h, row(attn_norm[layer]), moba_wqkv[j].astype(BF16),
                                  moba_cos, moba_sin, batch, seq)
            o = _moba_attn(q, k, vt)
            wo = moba_wo[j]
        else:
            wuq = mla_wuq[j].reshape(MLA_Q_RANK, HEADS, MLA_NOPE + MLA_ROPE)
            wuq = jnp.concatenate([wuq[:, :, :MLA_NOPE].reshape(MLA_Q_RANK, -1),
                                   wuq[:, :, MLA_NOPE:].reshape(MLA_Q_RANK, -1)], axis=1)
            wukv = mla_wukv[j].reshape(MLA_KV_RANK, HEADS, MLA_NOPE + HEAD_DIM)
            wukv = jnp.concatenate([wukv[:, :, :MLA_NOPE].reshape(MLA_KV_RANK, -1),
                                    wukv[:, :, MLA_NOPE:].reshape(MLA_KV_RANK, -1)], axis=1)
            wa = jnp.concatenate(
                [mla_wdq[j], mla_wdkv[j], jnp.zeros((d, 128 - MLA_ROPE), F32)], axis=1)
            q, k, vt = _mla_proj(h, row(attn_norm[layer]), wa.astype(BF16), row(mla_qnorm[j]),
                                 row(mla_kvnorm[j]), wuq.astype(BF16), wukv.astype(BF16),
                                 mla_cos, mla_slo, mla_shi, batch, seq)
            o = _causal_attn(q, k, vt)
            wo = mla_wo[j]

        wr = jnp.concatenate(
            [moe_wexpert[layer], moe_wgroup[layer],
             jnp.zeros((d, ROUTER_LANES - N_EXPERTS - N_GROUPS), F32)], axis=1)
        h1, hn, comb, grow = _post_attn(h, o.reshape(n, -1), wo.astype(BF16), row(ffn_norm[layer]),
                                        jnp.concatenate(_split_bf16(wr), axis=1))

        h = _moe_ple(hn, comb, grow, h1, p[layer].reshape(n, PLE_DIM), moe_wgate[layer].astype(BF16),
                     moe_wup[layer].astype(BF16),
                     moe_wdown[layer].reshape(N_GROUPS, PER_GROUP * EXPERT_FF, d).astype(BF16),
                     row(ple_norm[layer]), ple_gate[layer].astype(BF16),
                     ple_proj[layer].astype(BF16), row(final_norm), layer == N_LAYERS - 1)

    return h.reshape(batch, seq, d)
```

```python
import functools

import jax
import jax.numpy as jnp
from jax import lax
from jax.experimental import pallas as pl
from jax.experimental.pallas import tpu as pltpu

F32 = jnp.float32
BF16 = jnp.bfloat16

D_MODEL = 1024
N_LAYERS = 2
EPS = 1e-6
THETA = 10000.0
NEG = -1e30
LOG2E = 1.4426950408889634

HEADS = 8
HEAD_DIM = 128
MOBA_BLOCK = 256
MOBA_TOPK = 3

MLA_Q_RANK = 384
MLA_KV_RANK = 256
MLA_NOPE = 128
MLA_ROPE = 64
MLA_QK_PAD = 256

N_GROUPS = 4
PER_GROUP = 4
N_EXPERTS = 16
EXPERT_FF = 256
ROUTER_LANES = 128

PLE_DIM = 256

TILE = 512
Q_TILE = 1024
V_ROWS = HEAD_DIM + 16
ATTN_HEADS = 2
POST_TILE = 512
SORT_ALIGN = 16
SORT_ROWS = 640
MOE_CHUNK = 160

VMEM_LIMIT = 56 * 1024 * 1024


def _rms(x, g):
    return x * lax.rsqrt(jnp.mean(x * x, axis=-1, keepdims=True) + EPS) * g


def _nt_dot(a, b):
    return lax.dot_general(a, b, (((1,), (1,)), ((), ())), preferred_element_type=F32)


def _moba_proj_kernel(x_ref, g_ref, w_ref, cos_ref, sin_ref, qt_ref, k_ref, vt_ref):
    hn = _rms(x_ref[...], g_ref[...])
    qkv = jnp.dot(hn.astype(BF16), w_ref[...], preferred_element_type=F32)
    cos = cos_ref[...]
    sin = sin_ref[...]
    scale = HEAD_DIM ** -0.5 * LOG2E
    hd = HEADS * HEAD_DIM
    row = lax.broadcasted_iota(jnp.int32, (TILE, HEAD_DIM), 0)
    lane = lax.broadcasted_iota(jnp.int32, (TILE, HEAD_DIM), 1)
    block = pl.program_id(1) * (TILE // MOBA_BLOCK) + (row >> 8)
    block_onehot = jnp.where(lane == block, 1.0, 0.0).astype(BF16)
    for h in range(HEADS):
        lo = h * HEAD_DIM
        qh = qkv[:, lo:lo + HEAD_DIM]
        qh = qh * cos + pltpu.roll(qh, HEAD_DIM // 2, 1) * sin
        qt_ref[0, h] = (qh * scale).T.astype(BF16)
        kh = qkv[:, hd + lo:hd + lo + HEAD_DIM]
        kh = (kh * cos + pltpu.roll(kh, HEAD_DIM // 2, 1) * sin).astype(BF16)
        vh = qkv[:, 2 * hd + lo:2 * hd + lo + HEAD_DIM]
        _store_kv(k_ref, vt_ref, h, jnp.concatenate([kh, block_onehot], axis=1), vh)


def _store_kv(k_ref, vt_ref, h, kh, vh):
    k_ref[0, h, 0] = kh
    vt_ref[0, h, 0] = jnp.concatenate(
        [vh.T, jnp.ones((V_ROWS - HEAD_DIM, TILE), F32)], axis=0).astype(BF16)


def _qkv_out(batch, seq, dq, dk):
    steps = seq // TILE
    specs = [
        pl.BlockSpec((1, HEADS, dq, TILE), lambda b, t: (b, 0, 0, t)),
        pl.BlockSpec((1, HEADS, 1, TILE, dk), lambda b, t: (b, 0, t, 0, 0)),
        pl.BlockSpec((1, HEADS, 1, V_ROWS, TILE), lambda b, t: (b, 0, t, 0, 0)),
    ]
    shapes = [
        jax.ShapeDtypeStruct((batch, HEADS, dq, seq), BF16),
        jax.ShapeDtypeStruct((batch, HEADS, steps, TILE, dk), BF16),
        jax.ShapeDtypeStruct((batch, HEADS, steps, V_ROWS, TILE), BF16),
    ]
    return specs, shapes


def _moba_proj(x, g, w, cos2, sin2, batch, seq):
    tiles = seq // TILE
    out_specs, out_shape = _qkv_out(batch, seq, HEAD_DIM, 2 * HEAD_DIM)
    return pl.pallas_call(
        _moba_proj_kernel,
        grid=(batch, tiles),
        in_specs=[
            pl.BlockSpec((TILE, D_MODEL), lambda b, t: (b * tiles + t, 0)),
            pl.BlockSpec((1, D_MODEL), lambda b, t: (0, 0)),
            pl.BlockSpec((D_MODEL, 3 * HEADS * HEAD_DIM), lambda b, t: (0, 0)),
            pl.BlockSpec((TILE, HEAD_DIM), lambda b, t: (t, 0)),
            pl.BlockSpec((TILE, HEAD_DIM), lambda b, t: (t, 0)),
        ],
        out_specs=out_specs,
        out_shape=out_shape,
        compiler_params=pltpu.CompilerParams(
            dimension_semantics=("arbitrary", "arbitrary"), vmem_limit_bytes=VMEM_LIMIT),
        name="moba_proj",
    )(x, g, w, cos2, sin2)


def _attn_scratch(nh):
    return ([pltpu.VMEM((TILE, Q_TILE), F32)] * (2 * nh) + [pltpu.VMEM((1, Q_TILE), F32)] * (2 * nh)
            + [pltpu.VMEM((1, Q_TILE), F32)] * nh + [pltpu.VMEM((V_ROWS, Q_TILE), F32)] * nh)


def _attention(q_cols, k_ref, vt_ref, o_ref, scratch, i):
    nh, tq = k_ref.shape[1], o_ref.shape[1]
    assert tq == 2 * TILE
    s_refs = (scratch[:nh], scratch[nh:2 * nh])
    smax_refs = (scratch[2 * nh:3 * nh], scratch[3 * nh:4 * nh])
    m_refs = scratch[4 * nh:5 * nh]
    acc_refs = scratch[5 * nh:6 * nh]
    everything, left, right = slice(0, tq), slice(0, TILE), slice(TILE, tq)

    def produce(h, step, par, cols=everything):
        s = jnp.dot(k_ref[0, h, step], q_cols(h, cols), preferred_element_type=F32)
        s_refs[par][h][:, cols] = s
        smax_refs[par][h][:, cols] = jnp.max(s, axis=0, keepdims=True)

    def consume(h, step, par, cols=everything, causal=False):
        s = s_refs[par][h][:, cols]
        if causal:
            kpos = lax.broadcasted_iota(jnp.int32, s.shape, 0)
            qpos = lax.broadcasted_iota(jnp.int32, s.shape, 1)
            s = jnp.where(kpos <= qpos, s, NEG)
            s_max = jnp.max(s, axis=0, keepdims=True)
        else:
            s_max = smax_refs[par][h][:, cols]
        m = m_refs[h][:, cols]
        m_new = jnp.maximum(m, s_max)
        m_refs[h][:, cols] = m_new
        p = jnp.exp2(s - m_new).astype(BF16)
        pv = jnp.dot(vt_ref[0, h, step], p, preferred_element_type=F32)
        acc_refs[h][:, cols] = jnp.exp2(m - m_new) * acc_refs[h][:, cols] + pv

    for h in range(nh):
        m_refs[h][...] = jnp.full((1, tq), NEG, F32)
        acc_refs[h][...] = jnp.zeros((V_ROWS, tq), F32)
        produce(h, 0, 0)

    def body(u, _):
        for par in range(2):
            for h in range(nh):
                produce(h, 2 * u + par + 1, 1 - par)
                consume(h, 2 * u + par, par)
        return 0

    lax.fori_loop(0, i, body, 0)
    for h in range(nh):
        produce(h, 2 * i + 1, 1, right)
        consume(h, 2 * i, 0, left, causal=True)
        consume(h, 2 * i, 0, right)
    for h in range(nh):
        consume(h, 2 * i + 1, 1, right, causal=True)

    for h in range(nh):
        acc = acc_refs[h][...]
        out = acc[:HEAD_DIM] / acc[HEAD_DIM:HEAD_DIM + 1]
        o_ref[0, :, h * HEAD_DIM:(h + 1) * HEAD_DIM] = out.T.astype(BF16)


def _moba_attn_kernel(qt_ref, k_ref, vt_ref, o_ref, kmean_ref, qa_ref, *scratch):
    i = pl.program_id(2)
    nh, _, tq = qt_ref.shape[1:]
    steps = k_ref.shape[2]
    nb = TILE // MOBA_BLOCK
    nblk = steps * nb

    @pl.when(i == 0)
    def _():
        for h in range(nh):
            for n in range(nblk):
                kb = k_ref[0, h, n // nb, (n % nb) * MOBA_BLOCK:(n % nb + 1) * MOBA_BLOCK, :HEAD_DIM]
                kmean_ref[h, n:n + 1, :] = jnp.mean(kb.astype(F32), axis=0, keepdims=True)

    nidx = lax.broadcasted_iota(jnp.int32, (nblk, tq), 0)
    own = (i * tq + lax.broadcasted_iota(jnp.int32, (nblk, tq), 1)) >> 8
    past = nidx < own
    for h in range(nh):
        qt = qt_ref[0, h]
        km = kmean_ref[h]
        km_hi = km.astype(BF16)
        km_lo = (km - km_hi.astype(F32)).astype(BF16)
        gate = (jnp.dot(km_hi, qt, preferred_element_type=F32)
                + jnp.dot(km_lo, qt, preferred_element_type=F32))
        g1 = jnp.where(past, gate, -jnp.inf)
        thr = jnp.max(g1, axis=0, keepdims=True)
        g = g1
        for _ in range(MOBA_TOPK - 1):
            g = jnp.where(g >= thr, -jnp.inf, g)
            thr = jnp.max(g, axis=0, keepdims=True)
        keep = (past & (g1 >= thr)) | (nidx == own)
        qa_ref[h, :HEAD_DIM, :] = qt
        qa_ref[h, HEAD_DIM:HEAD_DIM + nblk, :] = jnp.where(keep, 0.0, NEG).astype(BF16)
        qa_ref[h, HEAD_DIM + nblk:, :] = jnp.zeros((HEAD_DIM - nblk, tq), BF16)

    _attention(lambda h, cols: qa_ref[h, :, cols], k_ref, vt_ref, o_ref, scratch, i)


def _causal_attn_kernel(qt_ref, k_ref, vt_ref, o_ref, *scratch):
    _attention(lambda h, cols: qt_ref[0, h, :, cols], k_ref, vt_ref, o_ref, scratch,
               pl.program_id(2))


def _attn_call(kernel_fn, name, qt, k, vt, scratch):
    batch, heads, dq, seq = qt.shape
    dk = k.shape[-1]
    steps = seq // TILE
    nh = ATTN_HEADS
    return pl.pallas_call(
        kernel_fn,
        grid=(batch, heads // nh, seq // Q_TILE),
        in_specs=[
            pl.BlockSpec((1, nh, dq, Q_TILE), lambda b, h, i: (b, h, 0, i)),
            pl.BlockSpec((1, nh, steps, TILE, dk), lambda b, h, i: (b, h, 0, 0, 0)),
            pl.BlockSpec((1, nh, steps, V_ROWS, TILE), lambda b, h, i: (b, h, 0, 0, 0)),
        ],
        out_specs=pl.BlockSpec((1, Q_TILE, nh * HEAD_DIM), lambda b, h, i: (b, i, h)),
        out_shape=jax.ShapeDtypeStruct((batch, seq, heads * HEAD_DIM), BF16),
        scratch_shapes=scratch + _attn_scratch(nh),
        compiler_params=pltpu.CompilerParams(
            dimension_semantics=("arbitrary", "arbitrary", "arbitrary"),
            vmem_limit_bytes=VMEM_LIMIT),
        name=name,
    )(qt, k, vt)


def _moba_attn(qt, k, vt):
    seq = qt.shape[3]
    scratch = [pltpu.VMEM((ATTN_HEADS, seq // MOBA_BLOCK, HEAD_DIM), F32),
               pltpu.VMEM((ATTN_HEADS, 2 * HEAD_DIM, Q_TILE), BF16)]
    return _attn_call(_moba_attn_kernel, "moba_attn", qt, k, vt, scratch)


def _causal_attn(qt, k, vt):
    return _attn_call(_causal_attn_kernel, "mla_attn", qt, k, vt, [])


def _rope_pairs32(x, cos, sin_lo, sin_hi):
    n = x.shape[1]
    return x * cos + pltpu.roll(x, n - 32, 1) * sin_lo + pltpu.roll(x, 32, 1) * sin_hi


def _mla_proj_kernel(x_ref, g_ref, wa_ref, qn_ref, kvn_ref, wuq_ref, wukv_ref,
                     cos_ref, slo_ref, shi_ref, qt_ref, k_ref, vt_ref):
    hn = _rms(x_ref[...], g_ref[...])
    r = jnp.dot(hn.astype(BF16), wa_ref[...], preferred_element_type=F32)
    cq = _rms(r[:, :MLA_Q_RANK], qn_ref[...])
    ckv = _rms(r[:, MLA_Q_RANK:MLA_Q_RANK + MLA_KV_RANK], kvn_ref[...])
    kr = r[:, MLA_Q_RANK + MLA_KV_RANK:]
    q = jnp.dot(cq.astype(BF16), wuq_ref[...], preferred_element_type=F32)
    kv = jnp.dot(ckv.astype(BF16), wukv_ref[...], preferred_element_type=F32)
    cos = cos_ref[...]
    slo = slo_ref[...]
    shi = shi_ref[...]
    nn = HEADS * MLA_NOPE
    reps = HEADS * MLA_ROPE // 128
    qr = _rope_pairs32(q[:, nn:], jnp.concatenate([cos] * reps, axis=1),
                       jnp.concatenate([slo] * reps, axis=1), jnp.concatenate([shi] * reps, axis=1))
    kr = _rope_pairs32(kr, cos, slo, shi)
    scale = (MLA_NOPE + MLA_ROPE) ** -0.5 * LOG2E
    low_half = lax.broadcasted_iota(jnp.int32, kr.shape, 1) < MLA_ROPE
    for h in range(HEADS):
        pair = qr[:, (h // 2) * 128:(h // 2 + 1) * 128]
        if h % 2:
            pair = pltpu.roll(pair, MLA_ROPE, 1)
        qh = jnp.concatenate(
            [q[:, h * MLA_NOPE:(h + 1) * MLA_NOPE], jnp.where(low_half, pair, 0.0)], axis=1)
        qt_ref[0, h] = (qh * scale).T.astype(BF16)
        kh = jnp.concatenate([kv[:, h * MLA_NOPE:(h + 1) * MLA_NOPE], kr], axis=1).astype(BF16)
        vh = kv[:, nn + h * HEAD_DIM:nn + (h + 1) * HEAD_DIM]
        _store_kv(k_ref, vt_ref, h, kh, vh)


def _mla_proj(x, g, wa, qn, kvn, wuq, wukv, cos, slo, shi, batch, seq):
    tiles = seq // TILE
    full = lambda a: pl.BlockSpec(a.shape, lambda b, t: (0,) * a.ndim)
    tab = pl.BlockSpec((TILE, 128), lambda b, t: (t, 0))
    out_specs, out_shape = _qkv_out(batch, seq, MLA_QK_PAD, MLA_QK_PAD)
    return pl.pallas_call(
        _mla_proj_kernel,
        grid=(batch, tiles),
        in_specs=[
            pl.BlockSpec((TILE, D_MODEL), lambda b, t: (b * tiles + t, 0)),
            full(g), full(wa), full(qn), full(kvn), full(wuq), full(wukv), tab, tab, tab,
        ],
        out_specs=out_specs,
        out_shape=out_shape,
        compiler_params=pltpu.CompilerParams(
            dimension_semantics=("arbitrary", "arbitrary"), vmem_limit_bytes=VMEM_LIMIT),
        name="mla_proj",
    )(x, g, wa, qn, kvn, wuq, wukv, cos, slo, shi)


def _post_attn_kernel(h_ref, o_ref, wo_ref, g_ref, wr_ref, h1_ref, hn_ref, comb_ref, grow_ref):
    h1 = h_ref[...] + jnp.dot(o_ref[...], wo_ref[...], preferred_element_type=F32)
    h1_ref[...] = h1
    hn = _rms(h1, g_ref[...])
    hi = hn.astype(BF16)
    hn_ref[...] = hi
    lo = (hn - hi.astype(F32)).astype(BF16)
    w2 = wr_ref[...]
    r_hi = jnp.dot(hi, w2, preferred_element_type=F32)
    r_lo = jnp.dot(lo, w2, preferred_element_type=F32)
    logits = ((r_hi[:, :ROUTER_LANES] + r_hi[:, ROUTER_LANES:])
              + (r_lo[:, :ROUTER_LANES] + r_lo[:, ROUTER_LANES:]))
    lane_i = lax.broadcasted_iota(jnp.int32, logits.shape, 1)
    lane = lane_i.astype(F32)
    is_group = (lane_i >= N_EXPERTS) & (lane_i < N_EXPERTS + N_GROUPS)

    def first_max(mask):
        v = jnp.max(jnp.where(mask, logits, -jnp.inf), axis=1, keepdims=True)
        idx = jnp.min(jnp.where(mask & (logits == v), lane, 1e9), axis=1, keepdims=True)
        return v, idx

    g_max, g_lane = first_max(is_group)
    g_sum = jnp.sum(jnp.where(is_group, jnp.exp(logits - g_max), 0.0), axis=1, keepdims=True)
    g_w = 1.0 / g_sum
    group_of_lane = (lane_i >> 2).astype(F32)
    in_group = (lane_i < N_EXPERTS) & (group_of_lane == (g_lane - N_EXPERTS))
    v1, i1 = first_max(in_group)
    v2, i2 = first_max(in_group & (lane != i1))
    t = jnp.exp(v2 - v1)
    w1 = g_w / (1.0 + t)
    w2 = w1 * t
    comb_ref[...] = jnp.where(lane == i1, w1, 0.0) + jnp.where(lane == i2, w2, 0.0)
    picked = jnp.where(lane == g_lane, 1.0, 0.0).astype(BF16)
    srow = lax.broadcasted_iota(jnp.int32, (8, ROUTER_LANES), 0)
    slane = lax.broadcasted_iota(jnp.int32, (8, ROUTER_LANES), 1)
    is_g = (srow == 0) & (slane >= N_EXPERTS) & (slane < N_EXPERTS + N_GROUPS)
    sel = jnp.where(is_g, (slane - N_EXPERTS).astype(F32), 0.0).astype(BF16)
    grow_ref[0] = _nt_dot(sel, picked)


def _post_attn(h, o, wo, g, wr):
    n = h.shape[0]
    tile = POST_TILE
    row = lambda w: pl.BlockSpec((tile, w), lambda t: (t, 0))
    full = lambda a: pl.BlockSpec(a.shape, lambda t: (0,) * a.ndim)
    return pl.pallas_call(
        _post_attn_kernel,
        grid=(n // tile,),
        in_specs=[row(D_MODEL), row(D_MODEL), full(wo), full(g), full(wr)],
        out_specs=[row(D_MODEL), row(D_MODEL), row(ROUTER_LANES),
                   pl.BlockSpec((1, 8, tile), lambda t: (t, 0, 0))],
        out_shape=[
            jax.ShapeDtypeStruct((n, D_MODEL), F32),
            jax.ShapeDtypeStruct((n, D_MODEL), BF16),
            jax.ShapeDtypeStruct((n, ROUTER_LANES), F32),
            jax.ShapeDtypeStruct((n // tile, 8, tile), F32),
        ],
        compiler_params=pltpu.CompilerParams(
            dimension_semantics=("arbitrary",), vmem_limit_bytes=VMEM_LIMIT),
        name="post_attn",
    )(h, o, wo, g, wr)


def _moe_ple_kernel(hn_ref, comb_ref, grow_ref, h1_ref, p_ref, tri_ref, wg_ref, wu_ref, wd_ref,
                    pn_ref, pg_ref, pp_ref, fn_ref, out_ref, xs_ref, cs_ref, ys_ref, *, final):
    t = hn_ref.shape[0]
    gid = grow_ref[0, 0:1, :]
    g_iota = lax.broadcasted_iota(jnp.int32, (8, t), 0).astype(F32)
    member = gid == g_iota
    onehot = jnp.where(member, 1.0, 0.0)
    rank = jnp.dot(onehot.astype(BF16), tri_ref[...], preferred_element_type=F32)
    starts, rows = [], []
    start = jnp.int32(0)
    for g in range(N_GROUPS):
        count = jnp.sum(jnp.where(gid == float(g), 1, 0).astype(jnp.int32))
        padded = ((count + (SORT_ALIGN - 1)) >> 4) << 4
        starts.append(start)
        rows.append(padded)
        start = start + padded
    start_vec = jnp.zeros((8, t), F32)
    for g in range(1, N_GROUPS):
        start_vec = jnp.where(g_iota == float(g), starts[g].astype(F32), start_vec)
    dest = jnp.sum(jnp.where(member, rank + start_vec, 0.0), axis=0, keepdims=True)
    r_iota = lax.broadcasted_iota(jnp.int32, (SORT_ROWS, t), 0).astype(F32)
    perm = jnp.where(r_iota == dest, 1.0, 0.0).astype(BF16)

    comb = comb_ref[...]
    comb_hi = comb.astype(BF16).astype(F32)
    comb_lo = (comb - comb_hi).astype(BF16).astype(F32)
    packed = (comb_hi + pltpu.roll(comb_lo, N_EXPERTS, 1)).astype(BF16)
    both = jnp.dot(perm, jnp.concatenate([hn_ref[...], packed], axis=1),
                   preferred_element_type=F32)
    xs_ref[0:SORT_ROWS, :] = both[:, :D_MODEL].astype(BF16)
    cs_ref[0:SORT_ROWS, :] = both[:, D_MODEL:]
    xs_ref[SORT_ROWS:, :] = jnp.zeros((MOE_CHUNK, D_MODEL), BF16)
    cs_ref[SORT_ROWS:, :] = jnp.zeros((MOE_CHUNK, ROUTER_LANES), F32)
    ys_ref[...] = jnp.zeros(ys_ref.shape, BF16)

    lane = lax.broadcasted_iota(jnp.int32, (MOE_CHUNK, ROUTER_LANES), 1)
    for g in range(N_GROUPS):
        def chunk(j, _, g=g):
            off = pl.multiple_of(starts[g] + j * MOE_CHUNK, SORT_ALIGN)
            xc = xs_ref[pl.ds(off, MOE_CHUNK), :]
            cc = cs_ref[pl.ds(off, MOE_CHUNK), :]
            parts = []
            for e in range(g * PER_GROUP, (g + 1) * PER_GROUP):
                a = jnp.dot(xc, wg_ref[e], preferred_element_type=F32)
                u = jnp.dot(xc, wu_ref[e], preferred_element_type=F32)
                c = jnp.sum(jnp.where((lane == e) | (lane == e + N_EXPERTS), cc, 0.0),
                            axis=1, keepdims=True)
                parts.append((a * jax.nn.sigmoid(a) * u * c).astype(BF16))
            y = jnp.dot(jnp.concatenate(parts, axis=1), wd_ref[g], preferred_element_type=F32)
            ys_ref[pl.ds(off, MOE_CHUNK), :] = y.astype(BF16)
            return 0

        n_chunks = sum((rows[g] > k * MOE_CHUNK).astype(jnp.int32)
                       for k in range(SORT_ROWS // MOE_CHUNK))
        lax.fori_loop(0, n_chunks, chunk, 0)

    moe = lax.dot_general(perm, ys_ref[0:SORT_ROWS, :], (((0,), (0,)), ((), ())),
                          preferred_element_type=F32)
    h2 = h1_ref[...] + moe
    gate = jax.nn.sigmoid(
        jnp.dot(_rms(h2, pn_ref[...]).astype(BF16), pg_ref[...], preferred_element_type=F32))
    proj = jnp.dot(p_ref[...].astype(BF16), pp_ref[...], preferred_element_type=F32)
    h3 = h2 + gate * proj
    out_ref[...] = _rms(h3, fn_ref[...]) if final else h3


def _moe_ple(hn, comb, grow, h1, p, wg, wu, wd, pn, pg, pp, fn, final):
    n = hn.shape[0]
    tile = POST_TILE
    row = lambda w: pl.BlockSpec((tile, w), lambda t: (t, 0))
    once = lambda a: pl.BlockSpec(a.shape, lambda t: (0,) * a.ndim, pipeline_mode=pl.Buffered(1))
    tri = (jnp.arange(tile)[:, None] < jnp.arange(tile)[None, :]).astype(BF16)
    return pl.pallas_call(
        functools.partial(_moe_ple_kernel, final=final),
        grid=(n // tile,),
        in_specs=[row(D_MODEL), row(ROUTER_LANES), pl.BlockSpec((1, 8, tile), lambda t: (t, 0, 0)),
                  row(D_MODEL), row(PLE_DIM), once(tri), once(wg), once(wu), once(wd),
                  once(pn), once(pg), once(pp), once(fn)],
        out_specs=row(D_MODEL),
        out_shape=jax.ShapeDtypeStruct((n, D_MODEL), F32),
        scratch_shapes=[pltpu.VMEM((SORT_ROWS + MOE_CHUNK, D_MODEL), BF16),
                        pltpu.VMEM((SORT_ROWS + MOE_CHUNK, ROUTER_LANES), F32),
                        pltpu.VMEM((SORT_ROWS + MOE_CHUNK, D_MODEL), BF16)],
        compiler_params=pltpu.CompilerParams(
            dimension_semantics=("arbitrary",), vmem_limit_bytes=VMEM_LIMIT),
        name="moe_ple",
    )(hn, comb, grow, h1, p, tri, wg, wu, wd, pn, pg, pp, fn)


def _rope_tables(seq, half):
    inv = THETA ** (-jnp.arange(half, dtype=F32) / half)
    ang = jnp.arange(seq, dtype=F32)[:, None] * inv[None, :]
    return jnp.cos(ang), jnp.sin(ang)


def _split_bf16(w):
    hi = w.astype(BF16)
    return hi, (w - hi.astype(F32)).astype(BF16)


def kernel(x, p, attn_norm, ffn_norm, ple_norm, final_norm, moba_wqkv, moba_wo, mla_wdq, mla_qnorm,
           mla_wuq, mla_wdkv, mla_kvnorm, mla_wukv, mla_wo, moe_wgroup, moe_wexpert, moe_wgate,
           moe_wup, moe_wdown, ple_gate, ple_proj):
    batch, seq, d = x.shape
    n = batch * seq
    h = x.reshape(n, d)

    cos64, sin64 = _rope_tables(seq, HEAD_DIM // 2)
    moba_cos = jnp.concatenate([cos64, cos64], axis=1)
    moba_sin = jnp.concatenate([-sin64, sin64], axis=1)
    cos32, sin32 = _rope_tables(seq, MLA_ROPE // 2)
    z32 = jnp.zeros_like(sin32)
    mla_cos = jnp.concatenate([cos32] * 4, axis=1)
    mla_slo = jnp.concatenate([-sin32, z32, -sin32, z32], axis=1)
    mla_shi = jnp.concatenate([z32, sin32, z32, sin32], axis=1)

    row = lambda v: v.reshape(1, -1)

    for layer in range(N_LAYERS):
        j = layer // 2
        if layer % 2 == 0:
            q, k, vt = _moba_proj(h, row(attn_norm[layer]), moba_wqkv[j].astype(BF16),
                                  moba_cos, moba_sin, batch, seq)
            o = _moba_attn(q, k, vt)
            wo = moba_wo[j]
        else:
            wuq = mla_wuq[j].reshape(MLA_Q_RANK, HEADS, MLA_NOPE + MLA_ROPE)
            wuq = jnp.concatenate([wuq[:, :, :MLA_NOPE].reshape(MLA_Q_RANK, -1),
                                   wuq[:, :, MLA_NOPE:].reshape(MLA_Q_RANK, -1)], axis=1)
            wukv = mla_wukv[j].reshape(MLA_KV_RANK, HEADS, MLA_NOPE + HEAD_DIM)
            wukv = jnp.concatenate([wukv[:, :, :MLA_NOPE].reshape(MLA_KV_RANK, -1),
                                    wukv[:, :, MLA_NOPE:].reshape(MLA_KV_RANK, -1)], axis=1)
            wa = jnp.concatenate(
                [mla_wdq[j], mla_wdkv[j], jnp.zeros((d, 128 - MLA_ROPE), F32)], axis=1)
            q, k, vt = _mla_proj(h, row(attn_norm[layer]), wa.astype(BF16), row(mla_qnorm[j]),
                                 row(mla_kvnorm[j]), wuq.astype(BF16), wukv.astype(BF16),
                                 mla_cos, mla_slo, mla_shi, batch, seq)
            o = _causal_attn(q, k, vt)
            wo = mla_wo[j]

        wr = jnp.concatenate(
            [moe_wexpert[layer], moe_wgroup[layer],
             jnp.zeros((d, ROUTER_LANES - N_EXPERTS - N_GROUPS), F32)], axis=1)
        h1, hn, comb, grow = _post_attn(h, o.reshape(n, -1), wo.astype(BF16), row(ffn_norm[layer]),
                                        jnp.concatenate(_split_bf16(wr), axis=1))

        h = _moe_ple(hn, comb, grow, h1, p[layer].reshape(n, PLE_DIM), moe_wgate[layer].astype(BF16),
                     moe_wup[layer].astype(BF16),
                     moe_wdown[layer].reshape(N_GROUPS, PER_GROUP * EXPERT_FF, d).astype(BF16),
                     row(ple_norm[layer]), ple_gate[layer].astype(BF16),
                     ple_proj[layer].astype(BF16), row(final_norm), layer == N_LAYERS - 1)

    return h.reshape(batch, seq, d)
```

```python
import functools

import jax
import jax.numpy as jnp
from jax import lax
from jax.experimental import pallas as pl
from jax.experimental.pallas import tpu as pltpu

F32 = jnp.float32
BF16 = jnp.bfloat16

D_MODEL = 1024
N_LAYERS = 2
EPS = 1e-6
THETA = 10000.0
NEG = -1e30
LOG2E = 1.4426950408889634

HEADS = 8
HEAD_DIM = 128
MOBA_BLOCK = 256
MOBA_TOPK = 3

MLA_Q_RANK = 384
MLA_KV_RANK = 256
MLA_NOPE = 128
MLA_ROPE = 64
MLA_QK_PAD = 256

N_GROUPS = 4
PER_GROUP = 4
N_EXPERTS = 16
EXPERT_FF = 256
ROUTER_LANES = 128

PLE_DIM = 256

TILE = 512
Q_TILE = 1024
V_ROWS = HEAD_DIM + 16
ATTN_HEADS = 2
POST_TILE = 512
SORT_ALIGN = 16
SORT_ROWS = 640
MOE_CHUNK = 160

VMEM_LIMIT = 56 * 1024 * 1024


def _rms(x, g):
    return x * lax.rsqrt(jnp.mean(x * x, axis=-1, keepdims=True) + EPS) * g


def _nt_dot(a, b):
    return lax.dot_general(a, b, (((1,), (1,)), ((), ())), preferred_element_type=F32)


def _moba_proj_kernel(x_ref, g_ref, w_ref, cos_ref, sin_ref, qt_ref, k_ref, vt_ref):
    hn = _rms(x_ref[...], g_ref[...])
    qkv = jnp.dot(hn.astype(BF16), w_ref[...], preferred_element_type=F32)
    cos = cos_ref[...]
    sin = sin_ref[...]
    scale = HEAD_DIM ** -0.5 * LOG2E
    hd = HEADS * HEAD_DIM
    row = lax.broadcasted_iota(jnp.int32, (TILE, HEAD_DIM), 0)
    lane = lax.broadcasted_iota(jnp.int32, (TILE, HEAD_DIM), 1)
    block = pl.program_id(1) * (TILE // MOBA_BLOCK) + (row >> 8)
    block_onehot = jnp.where(lane == block, 1.0, 0.0).astype(BF16)
    for h in range(HEADS):
        lo = h * HEAD_DIM
        qh = qkv[:, lo:lo + HEAD_DIM]
        qh = qh * cos + pltpu.roll(qh, HEAD_DIM // 2, 1) * sin
        qt_ref[0, h] = (qh * scale).T.astype(BF16)
        kh = qkv[:, hd + lo:hd + lo + HEAD_DIM]
        kh = (kh * cos + pltpu.roll(kh, HEAD_DIM // 2, 1) * sin).astype(BF16)
        vh = qkv[:, 2 * hd + lo:2 * hd + lo + HEAD_DIM]
        _store_kv(k_ref, vt_ref, h, jnp.concatenate([kh, block_onehot], axis=1), vh)


def _store_kv(k_ref, vt_ref, h, kh, vh):
    k_ref[0, h, 0] = kh
    vt_ref[0, h, 0] = jnp.concatenate(
        [vh.T, jnp.ones((V_ROWS - HEAD_DIM, TILE), F32)], axis=0).astype(BF16)


def _qkv_out(batch, seq, dq, dk):
    steps = seq // TILE
    specs = [
        pl.BlockSpec((1, HEADS, dq, TILE), lambda b, t: (b, 0, 0, t)),
        pl.BlockSpec((1, HEADS, 1, TILE, dk), lambda b, t: (b, 0, t, 0, 0)),
        pl.BlockSpec((1, HEADS, 1, V_ROWS, TILE), lambda b, t: (b, 0, t, 0, 0)),
    ]
    shapes = [
        jax.ShapeDtypeStruct((batch, HEADS, dq, seq), BF16),
        jax.ShapeDtypeStruct((batch, HEADS, steps, TILE, dk), BF16),
        jax.ShapeDtypeStruct((batch, HEADS, steps, V_ROWS, TILE), BF16),
    ]
    return specs, shapes


def _moba_proj(x, g, w, cos2, sin2, batch, seq):
    tiles = seq // TILE
    out_specs, out_shape = _qkv_out(batch, seq, HEAD_DIM, 2 * HEAD_DIM)
    return pl.pallas_call(
        _moba_proj_kernel,
        grid=(batch, tiles),
        in_specs=[
            pl.BlockSpec((TILE, D_MODEL), lambda b, t: (b * tiles + t, 0)),
            pl.BlockSpec((1, D_MODEL), lambda b, t: (0, 0)),
            pl.BlockSpec((D_MODEL, 3 * HEADS * HEAD_DIM), lambda b, t: (0, 0)),
            pl.BlockSpec((TILE, HEAD_DIM), lambda b, t: (t, 0)),
            pl.BlockSpec((TILE, HEAD_DIM), lambda b, t: (t, 0)),
        ],
        out_specs=out_specs,
        out_shape=out_shape,
        compiler_params=pltpu.CompilerParams(
            dimension_semantics=("arbitrary", "arbitrary"), vmem_limit_bytes=VMEM_LIMIT),
        name="moba_proj",
    )(x, g, w, cos2, sin2)


def _attn_scratch(nh):
    return ([pltpu.VMEM((TILE, Q_TILE), F32)] * (2 * nh) + [pltpu.VMEM((1, Q_TILE), F32)] * (2 * nh)
            + [pltpu.VMEM((1, Q_TILE), F32)] * nh + [pltpu.VMEM((V_ROWS, Q_TILE), F32)] * nh)


def _attention(q_cols, q_next, k_ref, vt_ref, o_ref, scratch, i):
    nh, tq = k_ref.shape[1], o_ref.shape[1]
    assert tq == 2 * TILE
    s_refs = (scratch[:nh], scratch[nh:2 * nh])
    smax_refs = (scratch[2 * nh:3 * nh], scratch[3 * nh:4 * nh])
    m_refs = scratch[4 * nh:5 * nh]
    acc_refs = scratch[5 * nh:6 * nh]
    everything, left, right = slice(0, tq), slice(0, TILE), slice(TILE, tq)

    def produce(h, step, par, cols=everything, q=q_cols):
        s = jnp.dot(k_ref[0, h, step], q(h, cols), preferred_element_type=F32)
        s_refs[par][h][:, cols] = s
        smax_refs[par][h][:, cols] = jnp.max(s, axis=0, keepdims=True)

    def consume(h, step, par, cols=everything, causal=False):
        s = s_refs[par][h][:, cols]
        if causal:
            kpos = lax.broadcasted_iota(jnp.int32, s.shape, 0)
            qpos = lax.broadcasted_iota(jnp.int32, s.shape, 1)
            s = jnp.where(kpos <= qpos, s, NEG)
            s_max = jnp.max(s, axis=0, keepdims=True)
        else:
            s_max = smax_refs[par][h][:, cols]
        m = m_refs[h][:, cols]
        m_new = jnp.maximum(m, s_max)
        m_refs[h][:, cols] = m_new
        p = jnp.exp2(s - m_new).astype(BF16)
        pv = jnp.dot(vt_ref[0, h, step], p, preferred_element_type=F32)
        acc_refs[h][:, cols] = jnp.exp2(m - m_new) * acc_refs[h][:, cols] + pv

    for h in range(nh):
        m_refs[h][...] = jnp.full((1, tq), NEG, F32)
        acc_refs[h][...] = jnp.zeros((V_ROWS, tq), F32)

    @pl.when(i == 0)
    def _():
        for h in range(nh):
            produce(h, 0, 0)

    def run(first, count):
        for c in range(count):
            for h in range(nh):
                produce(h, first + c + 1, 1 - c % 2)
                consume(h, first + c, c % 2)

    def body4(u, _):
        run(4 * u, 4)
        return 0

    def body2(u, _):
        run(4 * (i >> 1), 2)
        return 0

    lax.fori_loop(0, i >> 1, body4, 0)
    lax.fori_loop(0, i & 1, body2, 0)
    for h in range(nh):
        produce(h, 2 * i + 1, 1, right)
        consume(h, 2 * i, 0, left, causal=True)
        consume(h, 2 * i, 0, right)
    for h in range(nh):
        produce(h, 0, 0, q=q_next)
        consume(h, 2 * i + 1, 1, right, causal=True)

    for h in range(nh):
        acc = acc_refs[h][...]
        out = acc[:HEAD_DIM] / acc[HEAD_DIM:HEAD_DIM + 1]
        o_ref[0, :, h * HEAD_DIM:(h + 1) * HEAD_DIM] = out.T.astype(BF16)


def _moba_attn_kernel(qt_ref, qtn_ref, k_ref, vt_ref, o_ref, kmean_ref, qa_ref, *scratch):
    i = pl.program_id(2)
    nh, _, tq = qt_ref.shape[1:]
    steps = k_ref.shape[2]
    nb = TILE // MOBA_BLOCK
    nblk = steps * nb

    nidx = lax.broadcasted_iota(jnp.int32, (nblk, tq), 0)

    def build(h, qt, tile, slot):
        own = (tile * tq + lax.broadcasted_iota(jnp.int32, (nblk, tq), 1)) >> 8
        past = nidx < own
        km = kmean_ref[h]
        km_hi = km.astype(BF16)
        km_lo = (km - km_hi.astype(F32)).astype(BF16)
        gate = (jnp.dot(km_hi, qt, preferred_element_type=F32)
                + jnp.dot(km_lo, qt, preferred_element_type=F32))
        g1 = jnp.where(past, gate, -jnp.inf)
        thr = jnp.max(g1, axis=0, keepdims=True)
        g = g1
        for _ in range(MOBA_TOPK - 1):
            g = jnp.where(g >= thr, -jnp.inf, g)
            thr = jnp.max(g, axis=0, keepdims=True)
        keep = (past & (g1 >= thr)) | (nidx == own)
        qa_ref[slot, h, :HEAD_DIM, :] = qt
        qa_ref[slot, h, HEAD_DIM:HEAD_DIM + nblk, :] = jnp.where(keep, 0.0, NEG).astype(BF16)
        qa_ref[slot, h, HEAD_DIM + nblk:, :] = jnp.zeros((HEAD_DIM - nblk, tq), BF16)

    @pl.when(i == 0)
    def _():
        for h in range(nh):
            for n in range(nblk):
                kb = k_ref[0, h, n // nb, (n % nb) * MOBA_BLOCK:(n % nb + 1) * MOBA_BLOCK, :HEAD_DIM]
                kmean_ref[h, n:n + 1, :] = jnp.mean(kb.astype(F32), axis=0, keepdims=True)
        for h in range(nh):
            build(h, qt_ref[0, h], 0, 0)

    cur, nxt = i & 1, (i + 1) & 1
    for h in range(nh):
        build(h, qtn_ref[0, h], i + 1, nxt)

    _attention(lambda h, cols: qa_ref[cur, h, :, cols], lambda h, cols: qa_ref[nxt, h, :, cols],
               k_ref, vt_ref, o_ref, scratch, i)


def _causal_attn_kernel(qt_ref, qtn_ref, k_ref, vt_ref, o_ref, *scratch):
    _attention(lambda h, cols: qt_ref[0, h, :, cols], lambda h, cols: qtn_ref[0, h, :, cols],
               k_ref, vt_ref, o_ref, scratch, pl.program_id(2))


def _attn_call(kernel_fn, name, qt, k, vt, scratch):
    batch, heads, dq, seq = qt.shape
    dk = k.shape[-1]
    steps = seq // TILE
    n_q = seq // Q_TILE
    nh = ATTN_HEADS
    return pl.pallas_call(
        kernel_fn,
        grid=(batch, heads // nh, n_q),
        in_specs=[
            pl.BlockSpec((1, nh, dq, Q_TILE), lambda b, h, i: (b, h, 0, i)),
            pl.BlockSpec((1, nh, dq, Q_TILE), lambda b, h, i: (b, h, 0, jnp.minimum(i + 1, n_q - 1))),
            pl.BlockSpec((1, nh, steps, TILE, dk), lambda b, h, i: (b, h, 0, 0, 0)),
            pl.BlockSpec((1, nh, steps, V_ROWS, TILE), lambda b, h, i: (b, h, 0, 0, 0)),
        ],
        out_specs=pl.BlockSpec((1, Q_TILE, nh * HEAD_DIM), lambda b, h, i: (b, i, h)),
        out_shape=jax.ShapeDtypeStruct((batch, seq, heads * HEAD_DIM), BF16),
        scratch_shapes=scratch + _attn_scratch(nh),
        compiler_params=pltpu.CompilerParams(
            dimension_semantics=("arbitrary", "arbitrary", "arbitrary"),
            vmem_limit_bytes=VMEM_LIMIT),
        name=name,
    )(qt, qt, k, vt)


def _moba_attn(qt, k, vt):
    seq = qt.shape[3]
    scratch = [pltpu.VMEM((ATTN_HEADS, seq // MOBA_BLOCK, HEAD_DIM), F32),
               pltpu.VMEM((2, ATTN_HEADS, 2 * HEAD_DIM, Q_TILE), BF16)]
    return _attn_call(_moba_attn_kernel, "moba_attn", qt, k, vt, scratch)


def _causal_attn(qt, k, vt):
    return _attn_call(_causal_attn_kernel, "mla_attn", qt, k, vt, [])


def _rope_pairs32(x, cos, sin_lo, sin_hi):
    n = x.shape[1]
    return x * cos + pltpu.roll(x, n - 32, 1) * sin_lo + pltpu.roll(x, 32, 1) * sin_hi


def _mla_proj_kernel(x_ref, g_ref, wa_ref, qn_ref, kvn_ref, wuq_ref, wukv_ref,
                     cos_ref, slo_ref, shi_ref, qt_ref, k_ref, vt_ref):
    hn = _rms(x_ref[...], g_ref[...])
    r = jnp.dot(hn.astype(BF16), wa_ref[...], preferred_element_type=F32)
    cq = _rms(r[:, :MLA_Q_RANK], qn_ref[...])
    ckv = _rms(r[:, MLA_Q_RANK:MLA_Q_RANK + MLA_KV_RANK], kvn_ref[...])
    kr = r[:, MLA_Q_RANK + MLA_KV_RANK:]
    q = jnp.dot(cq.astype(BF16), wuq_ref[...], preferred_element_type=F32)
    kv = jnp.dot(ckv.astype(BF16), wukv_ref[...], preferred_element_type=F32)
    cos = cos_ref[...]
    slo = slo_ref[...]
    shi = shi_ref[...]
    nn = HEADS * MLA_NOPE
    reps = HEADS * MLA_ROPE // 128
    qr = _rope_pairs32(q[:, nn:], jnp.concatenate([cos] * reps, axis=1),
                       jnp.concatenate([slo] * reps, axis=1), jnp.concatenate([shi] * reps, axis=1))
    kr = _rope_pairs32(kr, cos, slo, shi)
    scale = (MLA_NOPE + MLA_ROPE) ** -0.5 * LOG2E
    low_half = lax.broadcasted_iota(jnp.int32, kr.shape, 1) < MLA_ROPE
    for h in range(HEADS):
        pair = qr[:, (h // 2) * 128:(h // 2 + 1) * 128]
        if h % 2:
            pair = pltpu.roll(pair, MLA_ROPE, 1)
        qh = jnp.concatenate(
            [q[:, h * MLA_NOPE:(h + 1) * MLA_NOPE], jnp.where(low_half, pair, 0.0)], axis=1)
        qt_ref[0, h] = (qh * scale).T.astype(BF16)
        kh = jnp.concatenate([kv[:, h * MLA_NOPE:(h + 1) * MLA_NOPE], kr], axis=1).astype(BF16)
        vh = kv[:, nn + h * HEAD_DIM:nn + (h + 1) * HEAD_DIM]
        _store_kv(k_ref, vt_ref, h, kh, vh)


def _mla_proj(x, g, wa, qn, kvn, wuq, wukv, cos, slo, shi, batch, seq):
    tiles = seq // TILE
    full = lambda a: pl.BlockSpec(a.shape, lambda b, t: (0,) * a.ndim)
    tab = pl.BlockSpec((TILE, 128), lambda b, t: (t, 0))
    out_specs, out_shape = _qkv_out(batch, seq, MLA_QK_PAD, MLA_QK_PAD)
    return pl.pallas_call(
        _mla_proj_kernel,
        grid=(batch, tiles),
        in_specs=[
            pl.BlockSpec((TILE, D_MODEL), lambda b, t: (b * tiles + t, 0)),
            full(g), full(wa), full(qn), full(kvn), full(wuq), full(wukv), tab, tab, tab,
        ],
        out_specs=out_specs,
        out_shape=out_shape,
        compiler_params=pltpu.CompilerParams(
            dimension_semantics=("arbitrary", "arbitrary"), vmem_limit_bytes=VMEM_LIMIT),
        name="mla_proj",
    )(x, g, wa, qn, kvn, wuq, wukv, cos, slo, shi)


def _post_attn_kernel(h_ref, o_ref, wo_ref, g_ref, wr_ref, h1_ref, hn_ref, comb_ref, grow_ref):
    h1 = h_ref[...] + jnp.dot(o_ref[...], wo_ref[...], preferred_element_type=F32)
    h1_ref[...] = h1
    hn = _rms(h1, g_ref[...])
    hi = hn.astype(BF16)
    hn_ref[...] = hi
    lo = (hn - hi.astype(F32)).astype(BF16)
    w2 = wr_ref[...]
    r_hi = jnp.dot(hi, w2, preferred_element_type=F32)
    r_lo = jnp.dot(lo, w2, preferred_element_type=F32)
    logits = ((r_hi[:, :ROUTER_LANES] + r_hi[:, ROUTER_LANES:])
              + (r_lo[:, :ROUTER_LANES] + r_lo[:, ROUTER_LANES:]))
    lane_i = lax.broadcasted_iota(jnp.int32, logits.shape, 1)
    lane = lane_i.astype(F32)
    is_group = (lane_i >= N_EXPERTS) & (lane_i < N_EXPERTS + N_GROUPS)

    def first_max(mask):
        v = jnp.max(jnp.where(mask, logits, -jnp.inf), axis=1, keepdims=True)
        idx = jnp.min(jnp.where(mask & (logits == v), lane, 1e9), axis=1, keepdims=True)
        return v, idx

    g_max, g_lane = first_max(is_group)
    g_sum = jnp.sum(jnp.where(is_group, jnp.exp(logits - g_max), 0.0), axis=1, keepdims=True)
    g_w = 1.0 / g_sum
    group_of_lane = (lane_i >> 2).astype(F32)
    in_group = (lane_i < N_EXPERTS) & (group_of_lane == (g_lane - N_EXPERTS))
    v1, i1 = first_max(in_group)
    v2, i2 = first_max(in_group & (lane != i1))
    t = jnp.exp(v2 - v1)
    w1 = g_w / (1.0 + t)
    w2 = w1 * t
    comb_ref[...] = jnp.where(lane == i1, w1, 0.0) + jnp.where(lane == i2, w2, 0.0)
    picked = jnp.where(lane == g_lane, 1.0, 0.0).astype(BF16)
    srow = lax.broadcasted_iota(jnp.int32, (8, ROUTER_LANES), 0)
    slane = lax.broadcasted_iota(jnp.int32, (8, ROUTER_LANES), 1)
    is_g = (srow == 0) & (slane >= N_EXPERTS) & (slane < N_EXPERTS + N_GROUPS)
    sel = jnp.where(is_g, (slane - N_EXPERTS).astype(F32), 0.0).astype(BF16)
    grow_ref[0] = _nt_dot(sel, picked)


def _post_attn(h, o, wo, g, wr):
    n = h.shape[0]
    tile = POST_TILE
    row = lambda w: pl.BlockSpec((tile, w), lambda t: (t, 0))
    full = lambda a: pl.BlockSpec(a.shape, lambda t: (0,) * a.ndim)
    return pl.pallas_call(
        _post_attn_kernel,
        grid=(n // tile,),
        in_specs=[row(D_MODEL), row(D_MODEL), full(wo), full(g), full(wr)],
        out_specs=[row(D_MODEL), row(D_MODEL), row(ROUTER_LANES),
                   pl.BlockSpec((1, 8, tile), lambda t: (t, 0, 0))],
        out_shape=[
            jax.ShapeDtypeStruct((n, D_MODEL), F32),
            jax.ShapeDtypeStruct((n, D_MODEL), BF16),
            jax.ShapeDtypeStruct((n, ROUTER_LANES), F32),
            jax.ShapeDtypeStruct((n // tile, 8, tile), F32),
        ],
        compiler_params=pltpu.CompilerParams(
            dimension_semantics=("arbitrary",), vmem_limit_bytes=VMEM_LIMIT),
        name="post_attn",
    )(h, o, wo, g, wr)


def _moe_ple_kernel(hn_ref, comb_ref, grow_ref, h1_ref, p_ref, tri_ref, wg_ref, wu_ref, wd_ref,
                    pn_ref, pg_ref, pp_ref, fn_ref, out_ref, xs_ref, cs_ref, ys_ref, *, final):
    t = hn_ref.shape[0]
    gid = grow_ref[0, 0:1, :]
    g_iota = lax.broadcasted_iota(jnp.int32, (8, t), 0).astype(F32)
    member = gid == g_iota
    onehot = jnp.where(member, 1.0, 0.0)
    rank = jnp.dot(onehot.astype(BF16), tri_ref[...], preferred_element_type=F32)
    starts, rows = [], []
    start = jnp.int32(0)
    for g in range(N_GROUPS):
        count = jnp.sum(jnp.where(gid == float(g), 1, 0).astype(jnp.int32))
        padded = ((count + (SORT_ALIGN - 1)) >> 4) << 4
        starts.append(start)
        rows.append(padded)
        start = start + padded
    start_vec = jnp.zeros((8, t), F32)
    for g in range(1, N_GROUPS):
        start_vec = jnp.where(g_iota == float(g), starts[g].astype(F32), start_vec)
    dest = jnp.sum(jnp.where(member, rank + start_vec, 0.0), axis=0, keepdims=True)
    r_iota = lax.broadcasted_iota(jnp.int32, (SORT_ROWS, t), 0).astype(F32)
    perm = jnp.where(r_iota == dest, 1.0, 0.0).astype(BF16)

    comb = comb_ref[...]
    comb_hi = comb.astype(BF16).astype(F32)
    comb_lo = (comb - comb_hi).astype(BF16).astype(F32)
    packed = (comb_hi + pltpu.roll(comb_lo, N_EXPERTS, 1)).astype(BF16)
    both = jnp.dot(perm, jnp.concatenate([hn_ref[...], packed], axis=1),
                   preferred_element_type=F32)
    xs_ref[0:SORT_ROWS, :] = both[:, :D_MODEL].astype(BF16)
    cs_ref[0:SORT_ROWS, :] = both[:, D_MODEL:]
    xs_ref[SORT_ROWS:, :] = jnp.zeros((MOE_CHUNK, D_MODEL), BF16)
    cs_ref[SORT_ROWS:, :] = jnp.zeros((MOE_CHUNK, ROUTER_LANES), F32)
    ys_ref[...] = jnp.zeros(ys_ref.shape, BF16)

    lane = lax.broadcasted_iota(jnp.int32, (MOE_CHUNK, ROUTER_LANES), 1)
    for g in range(N_GROUPS):
        def chunk(j, _, g=g):
            off = pl.multiple_of(starts[g] + j * MOE_CHUNK, SORT_ALIGN)
            xc = xs_ref[pl.ds(off, MOE_CHUNK), :]
            cc = cs_ref[pl.ds(off, MOE_CHUNK), :]
            parts = []
            for e in range(g * PER_GROUP, (g + 1) * PER_GROUP):
                a = jnp.dot(xc, wg_ref[e], preferred_element_type=F32)
                u = jnp.dot(xc, wu_ref[e], preferred_element_type=F32)
                c = jnp.sum(jnp.where((lane == e) | (lane == e + N_EXPERTS), cc, 0.0),
                            axis=1, keepdims=True)
                parts.append((a * jax.nn.sigmoid(a) * u * c).astype(BF16))
            y = jnp.dot(jnp.concatenate(parts, axis=1), wd_ref[g], preferred_element_type=F32)
            ys_ref[pl.ds(off, MOE_CHUNK), :] = y.astype(BF16)
            return 0

        n_chunks = sum((rows[g] > k * MOE_CHUNK).astype(jnp.int32)
                       for k in range(SORT_ROWS // MOE_CHUNK))
        lax.fori_loop(0, n_chunks, chunk, 0)

    moe = lax.dot_general(perm, ys_ref[0:SORT_ROWS, :], (((0,), (0,)), ((), ())),
                          preferred_element_type=F32)
    h2 = h1_ref[...] + moe
    gate = jax.nn.sigmoid(
        jnp.dot(_rms(h2, pn_ref[...]).astype(BF16), pg_ref[...], preferred_element_type=F32))
    proj = jnp.dot(p_ref[...].astype(BF16), pp_ref[...], preferred_element_type=F32)
    h3 = h2 + gate * proj
    out_ref[...] = _rms(h3, fn_ref[...]) if final else h3


def _moe_ple(hn, comb, grow, h1, p, wg, wu, wd, pn, pg, pp, fn, final):
    n = hn.shape[0]
    tile = POST_TILE
    row = lambda w: pl.BlockSpec((tile, w), lambda t: (t, 0))
    once = lambda a: pl.BlockSpec(a.shape, lambda t: (0,) * a.ndim, pipeline_mode=pl.Buffered(1))
    tri = (jnp.arange(tile)[:, None] < jnp.arange(tile)[None, :]).astype(BF16)
    return pl.pallas_call(
        functools.partial(_moe_ple_kernel, final=final),
        grid=(n // tile,),
        in_specs=[row(D_MODEL), row(ROUTER_LANES), pl.BlockSpec((1, 8, tile), lambda t: (t, 0, 0)),
                  row(D_MODEL), row(PLE_DIM), once(tri), once(wg), once(wu), once(wd),
                  once(pn), once(pg), once(pp), once(fn)],
        out_specs=row(D_MODEL),
        out_shape=jax.ShapeDtypeStruct((n, D_MODEL), F32),
        scratch_shapes=[pltpu.VMEM((SORT_ROWS + MOE_CHUNK, D_MODEL), BF16),
                        pltpu.VMEM((SORT_ROWS + MOE_CHUNK, ROUTER_LANES), F32),
                        pltpu.VMEM((SORT_ROWS + MOE_CHUNK, D_MODEL), BF16)],
        compiler_params=pltpu.CompilerParams(
            dimension_semantics=("arbitrary",), vmem_limit_bytes=VMEM_LIMIT),
        name="moe_ple",
    )(hn, comb, grow, h1, p, tri, wg, wu, wd, pn, pg, pp, fn)


def _rope_tables(seq, half):
    inv = THETA ** (-jnp.arange(half, dtype=F32) / half)
    ang = jnp.arange(seq, dtype=F32)[:, None] * inv[None, :]
    return jnp.cos(ang), jnp.sin(ang)


def _split_bf16(w):
    hi = w.astype(BF16)
    return hi, (w - hi.astype(F32)).astype(BF16)


def kernel(x, p, attn_norm, ffn_norm, ple_norm, final_norm, moba_wqkv, moba_wo, mla_wdq, mla_qnorm,
           mla_wuq, mla_wdkv, mla_kvnorm, mla_wukv, mla_wo, moe_wgroup, moe_wexpert, moe_wgate,
           moe_wup, moe_wdown, ple_gate, ple_proj):
    batch, seq, d = x.shape
    n = batch * seq
    h = x.reshape(n, d)

    cos64, sin64 = _rope_tables(seq, HEAD_DIM // 2)
    moba_cos = jnp.concatenate([cos64, cos64], axis=1)
    moba_sin = jnp.concatenate([-sin64, sin64], axis=1)
    cos32, sin32 = _rope_tables(seq, MLA_ROPE // 2)
    z32 = jnp.zeros_like(sin32)
    mla_cos = jnp.concatenate([cos32] * 4, axis=1)
    mla_slo = jnp.concatenate([-sin32, z32, -sin32, z32], axis=1)
    mla_shi = jnp.concatenate([z32, sin32, z32, sin32], axis=1)

    row = lambda v: v.reshape(1, -1)

    for layer in range(N_LAYERS):
        j = layer // 2
        if layer % 2 == 0:
            q, k, vt = _moba_proj(h, row(attn_norm[layer]), moba_wqkv[j].astype(BF16),
                                  moba_cos, moba_sin, batch, seq)
            o = _moba_attn(q, k, vt)
            wo = moba_wo[j]
        else:
            wuq = mla_wuq[j].reshape(MLA_Q_RANK, HEADS, MLA_NOPE + MLA_ROPE)
            wuq = jnp.concatenate([wuq[:, :, :MLA_NOPE].reshape(MLA_Q_RANK, -1),
                                   wuq[:, :, MLA_NOPE:].reshape(MLA_Q_RANK, -1)], axis=1)
            wukv = mla_wukv[j].reshape(MLA_KV_RANK, HEADS, MLA_NOPE + HEAD_DIM)
            wukv = jnp.concatenate([wukv[:, :, :MLA_NOPE].reshape(MLA_KV_RANK, -1),
                                    wukv[:, :, MLA_NOPE:].reshape(MLA_KV_RANK, -1)], axis=1)
            wa = jnp.concatenate(
                [mla_wdq[j], mla_wdkv[j], jnp.zeros((d, 128 - MLA_ROPE), F32)], axis=1)
            q, k, vt = _mla_proj(h, row(attn_norm[layer]), wa.astype(BF16), row(mla_qnorm[j]),
                                 row(mla_kvnorm[j]), wuq.astype(BF16), wukv.astype(BF16),
                                 mla_cos, mla_slo, mla_shi, batch, seq)
            o = _causal_attn(q, k, vt)
            wo = mla_wo[j]

        wr = jnp.concatenate(
            [moe_wexpert[layer], moe_wgroup[layer],
             jnp.zeros((d, ROUTER_LANES - N_EXPERTS - N_GROUPS), F32)], axis=1)
        h1, hn, comb, grow = _post_attn(h, o.reshape(n, -1), wo.astype(BF16), row(ffn_norm[layer]),
                                        jnp.concatenate(_split_bf16(wr), axis=1))

        h = _moe_ple(hn, comb, grow, h1, p[layer].reshape(n, PLE_DIM), moe_wgate[layer].astype(BF16),
                     moe_wup[layer].astype(BF16),
                     moe_wdown[layer].reshape(N_GROUPS, PER_GROUP * EXPERT_FF, d).astype(BF16),
                     row(ple_norm[layer]), ple_gate[layer].astype(BF16),
                     ple_proj[layer].astype(BF16), row(final_norm), layer == N_LAYERS - 1)

    return h.reshape(batch, seq, d)
```

```python
import functools

import numpy as np
import jax
import jax.numpy as jnp
from jax import lax
from jax.experimental import pallas as pl
from jax.experimental.pallas import tpu as pltpu

F32 = jnp.float32
BF16 = jnp.bfloat16

D_MODEL = 1024
N_LAYERS = 2
EPS = 1e-6
THETA = 10000.0
NEG = -1e30
LOG2E = 1.4426950408889634

HEADS = 8
HEAD_DIM = 128
MOBA_BLOCK = 256
MOBA_TOPK = 3

MLA_Q_RANK = 384
MLA_KV_RANK = 256
MLA_NOPE = 128
MLA_ROPE = 64
MLA_QK_PAD = 256

N_GROUPS = 4
PER_GROUP = 4
N_EXPERTS = 16
EXPERT_FF = 256
ROUTER_LANES = 128

PLE_DIM = 256

TILE = 512
Q_TILE = 1024
V_ROWS = HEAD_DIM + 16
ATTN_HEADS = 2
POST_TILE = 512
SORT_ALIGN = 16
SORT_ROWS = 640
MOE_CHUNK = 144

VMEM_LIMIT = 56 * 1024 * 1024


def _rms(x, g):
    return x * lax.rsqrt(jnp.mean(x * x, axis=-1, keepdims=True) + EPS) * g


def _nt_dot(a, b):
    return lax.dot_general(a, b, (((1,), (1,)), ((), ())), preferred_element_type=F32)


def _moba_proj_kernel(x_ref, g_ref, w_ref, cos_ref, sin_ref, qt_ref, k_ref, vt_ref):
    hn = _rms(x_ref[...], g_ref[...])
    qkv = jnp.dot(hn.astype(BF16), w_ref[...], preferred_element_type=F32)
    cos = cos_ref[...]
    sin = sin_ref[...]
    scale = HEAD_DIM ** -0.5 * LOG2E
    hd = HEADS * HEAD_DIM
    row = lax.broadcasted_iota(jnp.int32, (TILE, HEAD_DIM), 0)
    lane = lax.broadcasted_iota(jnp.int32, (TILE, HEAD_DIM), 1)
    block = pl.program_id(1) * (TILE // MOBA_BLOCK) + (row >> 8)
    block_onehot = jnp.where(lane == block, 1.0, 0.0).astype(BF16)
    for h in range(HEADS):
        lo = h * HEAD_DIM
        qh = qkv[:, lo:lo + HEAD_DIM]
        qh = qh * cos + pltpu.roll(qh, HEAD_DIM // 2, 1) * sin
        qt_ref[0, h] = (qh * scale).T.astype(BF16)
        kh = qkv[:, hd + lo:hd + lo + HEAD_DIM]
        kh = (kh * cos + pltpu.roll(kh, HEAD_DIM // 2, 1) * sin).astype(BF16)
        vh = qkv[:, 2 * hd + lo:2 * hd + lo + HEAD_DIM]
        _store_kv(k_ref, vt_ref, h, jnp.concatenate([kh, block_onehot], axis=1), vh)


def _store_kv(k_ref, vt_ref, h, kh, vh):
    k_ref[0, h, 0] = kh
    vt_ref[0, h, 0] = jnp.concatenate(
        [vh.T, jnp.ones((V_ROWS - HEAD_DIM, TILE), F32)], axis=0).astype(BF16)


def _qkv_out(batch, seq, dq, dk):
    steps = seq // TILE
    specs = [
        pl.BlockSpec((1, HEADS, dq, TILE), lambda b, t: (b, 0, 0, t)),
        pl.BlockSpec((1, HEADS, 1, TILE, dk), lambda b, t: (b, 0, t, 0, 0)),
        pl.BlockSpec((1, HEADS, 1, V_ROWS, TILE), lambda b, t: (b, 0, t, 0, 0)),
    ]
    shapes = [
        jax.ShapeDtypeStruct((batch, HEADS, dq, seq), BF16),
        jax.ShapeDtypeStruct((batch, HEADS, steps, TILE, dk), BF16),
        jax.ShapeDtypeStruct((batch, HEADS, steps, V_ROWS, TILE), BF16),
    ]
    return specs, shapes


def _moba_proj(x, g, w, cos2, sin2, batch, seq):
    tiles = seq // TILE
    out_specs, out_shape = _qkv_out(batch, seq, HEAD_DIM, 2 * HEAD_DIM)
    return pl.pallas_call(
        _moba_proj_kernel,
        grid=(batch, tiles),
        in_specs=[
            pl.BlockSpec((TILE, D_MODEL), lambda b, t: (b * tiles + t, 0)),
            pl.BlockSpec((1, D_MODEL), lambda b, t: (0, 0)),
            pl.BlockSpec((D_MODEL, 3 * HEADS * HEAD_DIM), lambda b, t: (0, 0)),
            pl.BlockSpec((TILE, HEAD_DIM), lambda b, t: (t, 0)),
            pl.BlockSpec((TILE, HEAD_DIM), lambda b, t: (t, 0)),
        ],
        out_specs=out_specs,
        out_shape=out_shape,
        compiler_params=pltpu.CompilerParams(
            dimension_semantics=("arbitrary", "arbitrary"), vmem_limit_bytes=VMEM_LIMIT),
        name="moba_proj",
    )(x, g, w, cos2, sin2)


def _attn_scratch(nh):
    return ([pltpu.VMEM((TILE, Q_TILE), F32)] * (2 * nh) + [pltpu.VMEM((1, Q_TILE), F32)] * (2 * nh)
            + [pltpu.VMEM((1, Q_TILE), F32)] * nh + [pltpu.VMEM((V_ROWS, Q_TILE), F32)] * nh)


def _attention(q_cols, q_next, k_ref, vt_ref, o_ref, scratch, i):
    nh, tq = k_ref.shape[1], o_ref.shape[1]
    assert tq == 2 * TILE
    s_refs = (scratch[:nh], scratch[nh:2 * nh])
    smax_refs = (scratch[2 * nh:3 * nh], scratch[3 * nh:4 * nh])
    m_refs = scratch[4 * nh:5 * nh]
    acc_refs = scratch[5 * nh:6 * nh]
    everything, left, right = slice(0, tq), slice(0, TILE), slice(TILE, tq)

    def produce(h, step, par, cols=everything, q=q_cols):
        s = jnp.dot(k_ref[0, h, step], q(h, cols), preferred_element_type=F32)
        s_refs[par][h][:, cols] = s
        smax_refs[par][h][:, cols] = jnp.max(s, axis=0, keepdims=True)

    def consume(h, step, par, cols=everything, causal=False):
        s = s_refs[par][h][:, cols]
        if causal:
            kpos = lax.broadcasted_iota(jnp.int32, s.shape, 0)
            qpos = lax.broadcasted_iota(jnp.int32, s.shape, 1)
            s = jnp.where(kpos <= qpos, s, NEG)
            s_max = jnp.max(s, axis=0, keepdims=True)
        else:
            s_max = smax_refs[par][h][:, cols]
        m = m_refs[h][:, cols]
        m_new = jnp.maximum(m, s_max)
        m_refs[h][:, cols] = m_new
        p = jnp.exp2(s - m_new).astype(BF16)
        pv = jnp.dot(vt_ref[0, h, step], p, preferred_element_type=F32)
        acc_refs[h][:, cols] = jnp.exp2(m - m_new) * acc_refs[h][:, cols] + pv

    for h in range(nh):
        m_refs[h][...] = jnp.full((1, tq), NEG, F32)
        acc_refs[h][...] = jnp.zeros((V_ROWS, tq), F32)

    @pl.when(i == 0)
    def _():
        for h in range(nh):
            produce(h, 0, 0)

    def run(first, count):
        for c in range(count):
            for h in range(nh):
                produce(h, first + c + 1, 1 - c % 2)
                consume(h, first + c, c % 2)

    def body4(u, _):
        run(4 * u, 4)
        return 0

    def body2(u, _):
        run(4 * (i >> 1), 2)
        return 0

    lax.fori_loop(0, i >> 1, body4, 0)
    lax.fori_loop(0, i & 1, body2, 0)
    for h in range(nh):
        produce(h, 2 * i + 1, 1, right)
        consume(h, 2 * i, 0, left, causal=True)
        consume(h, 2 * i, 0, right)
    for h in range(nh):
        produce(h, 0, 0, q=q_next)
        consume(h, 2 * i + 1, 1, right, causal=True)

    for h in range(nh):
        acc = acc_refs[h][...]
        out = acc[:HEAD_DIM] / acc[HEAD_DIM:HEAD_DIM + 1]
        o_ref[0, :, h * HEAD_DIM:(h + 1) * HEAD_DIM] = out.T.astype(BF16)


def _moba_attn_kernel(qt_ref, qtn_ref, k_ref, vt_ref, o_ref, kmean_ref, qa_ref, *scratch):
    i = pl.program_id(2)
    nh, _, tq = qt_ref.shape[1:]
    steps = k_ref.shape[2]
    nb = TILE // MOBA_BLOCK
    nblk = steps * nb

    nidx = lax.broadcasted_iota(jnp.int32, (nblk, tq), 0)

    def build(h, qt, tile, slot):
        own = (tile * tq + lax.broadcasted_iota(jnp.int32, (nblk, tq), 1)) >> 8
        past = nidx < own
        km = kmean_ref[h]
        km_hi = km.astype(BF16)
        km_lo = (km - km_hi.astype(F32)).astype(BF16)
        gate = (jnp.dot(km_hi, qt, preferred_element_type=F32)
                + jnp.dot(km_lo, qt, preferred_element_type=F32))
        g1 = jnp.where(past, gate, -jnp.inf)
        thr = jnp.max(g1, axis=0, keepdims=True)
        g = g1
        for _ in range(MOBA_TOPK - 1):
            g = jnp.where(g >= thr, -jnp.inf, g)
            thr = jnp.max(g, axis=0, keepdims=True)
        keep = (past & (g1 >= thr)) | (nidx == own)
        qa_ref[slot, h, :HEAD_DIM, :] = qt
        qa_ref[slot, h, HEAD_DIM:HEAD_DIM + nblk, :] = jnp.where(keep, 0.0, NEG).astype(BF16)
        qa_ref[slot, h, HEAD_DIM + nblk:, :] = jnp.zeros((HEAD_DIM - nblk, tq), BF16)

    @pl.when(i == 0)
    def _():
        for h in range(nh):
            for n in range(nblk):
                kb = k_ref[0, h, n // nb, (n % nb) * MOBA_BLOCK:(n % nb + 1) * MOBA_BLOCK, :HEAD_DIM]
                kmean_ref[h, n:n + 1, :] = jnp.mean(kb.astype(F32), axis=0, keepdims=True)
        for h in range(nh):
            build(h, qt_ref[0, h], 0, 0)

    cur, nxt = i & 1, (i + 1) & 1
    for h in range(nh):
        build(h, qtn_ref[0, h], i + 1, nxt)

    _attention(lambda h, cols: qa_ref[cur, h, :, cols], lambda h, cols: qa_ref[nxt, h, :, cols],
               k_ref, vt_ref, o_ref, scratch, i)


def _causal_attn_kernel(qt_ref, qtn_ref, k_ref, vt_ref, o_ref, *scratch):
    _attention(lambda h, cols: qt_ref[0, h, :, cols], lambda h, cols: qtn_ref[0, h, :, cols],
               k_ref, vt_ref, o_ref, scratch, pl.program_id(2))


def _attn_call(kernel_fn, name, qt, k, vt, scratch):
    batch, heads, dq, seq = qt.shape
    dk = k.shape[-1]
    steps = seq // TILE
    n_q = seq // Q_TILE
    nh = ATTN_HEADS
    return pl.pallas_call(
        kernel_fn,
        grid=(batch, heads // nh, n_q),
        in_specs=[
            pl.BlockSpec((1, nh, dq, Q_TILE), lambda b, h, i: (b, h, 0, i)),
            pl.BlockSpec((1, nh, dq, Q_TILE), lambda b, h, i: (b, h, 0, jnp.minimum(i + 1, n_q - 1))),
            pl.BlockSpec((1, nh, steps, TILE, dk), lambda b, h, i: (b, h, 0, 0, 0)),
            pl.BlockSpec((1, nh, steps, V_ROWS, TILE), lambda b, h, i: (b, h, 0, 0, 0)),
        ],
        out_specs=pl.BlockSpec((1, Q_TILE, nh * HEAD_DIM), lambda b, h, i: (b, i, h)),
        out_shape=jax.ShapeDtypeStruct((batch, seq, heads * HEAD_DIM), BF16),
        scratch_shapes=scratch + _attn_scratch(nh),
        compiler_params=pltpu.CompilerParams(
            dimension_semantics=("arbitrary", "arbitrary", "arbitrary"),
            vmem_limit_bytes=VMEM_LIMIT),
        name=name,
    )(qt, qt, k, vt)


def _moba_attn(qt, k, vt):
    seq = qt.shape[3]
    scratch = [pltpu.VMEM((ATTN_HEADS, seq // MOBA_BLOCK, HEAD_DIM), F32),
               pltpu.VMEM((2, ATTN_HEADS, 2 * HEAD_DIM, Q_TILE), BF16)]
    return _attn_call(_moba_attn_kernel, "moba_attn", qt, k, vt, scratch)


def _causal_attn(qt, k, vt):
    return _attn_call(_causal_attn_kernel, "mla_attn", qt, k, vt, [])


def _rope_pairs32(x, cos, sin_lo, sin_hi):
    n = x.shape[1]
    return x * cos + pltpu.roll(x, n - 32, 1) * sin_lo + pltpu.roll(x, 32, 1) * sin_hi


def _mla_proj_kernel(x_ref, g_ref, wa_ref, qn_ref, kvn_ref, wuq_ref, wukv_ref,
                     cos_ref, slo_ref, shi_ref, qt_ref, k_ref, vt_ref):
    hn = _rms(x_ref[...], g_ref[...])
    r = jnp.dot(hn.astype(BF16), wa_ref[...], preferred_element_type=F32)
    cq = _rms(r[:, :MLA_Q_RANK], qn_ref[...])
    ckv = _rms(r[:, MLA_Q_RANK:MLA_Q_RANK + MLA_KV_RANK], kvn_ref[...])
    kr = r[:, MLA_Q_RANK + MLA_KV_RANK:]
    q = jnp.dot(cq.astype(BF16), wuq_ref[...], preferred_element_type=F32)
    kv = jnp.dot(ckv.astype(BF16), wukv_ref[...], preferred_element_type=F32)
    cos = cos_ref[...]
    slo = slo_ref[...]
    shi = shi_ref[...]
    nn = HEADS * MLA_NOPE
    reps = HEADS * MLA_ROPE // 128
    qr = _rope_pairs32(q[:, nn:], jnp.concatenate([cos] * reps, axis=1),
                       jnp.concatenate([slo] * reps, axis=1), jnp.concatenate([shi] * reps, axis=1))
    kr = _rope_pairs32(kr, cos, slo, shi)
    scale = (MLA_NOPE + MLA_ROPE) ** -0.5 * LOG2E
    low_half = lax.broadcasted_iota(jnp.int32, kr.shape, 1) < MLA_ROPE
    for h in range(HEADS):
        pair = qr[:, (h // 2) * 128:(h // 2 + 1) * 128]
        if h % 2:
            pair = pltpu.roll(pair, MLA_ROPE, 1)
        qh = jnp.concatenate(
            [q[:, h * MLA_NOPE:(h + 1) * MLA_NOPE], jnp.where(low_half, pair, 0.0)], axis=1)
        qt_ref[0, h] = (qh * scale).T.astype(BF16)
        kh = jnp.concatenate([kv[:, h * MLA_NOPE:(h + 1) * MLA_NOPE], kr], axis=1).astype(BF16)
        vh = kv[:, nn + h * HEAD_DIM:nn + (h + 1) * HEAD_DIM]
        _store_kv(k_ref, vt_ref, h, kh, vh)


def _mla_proj(x, g, wa, qn, kvn, wuq, wukv, cos, slo, shi, batch, seq):
    tiles = seq // TILE
    full = lambda a: pl.BlockSpec(a.shape, lambda b, t: (0,) * a.ndim)
    tab = pl.BlockSpec((TILE, 128), lambda b, t: (t, 0))
    out_specs, out_shape = _qkv_out(batch, seq, MLA_QK_PAD, MLA_QK_PAD)
    return pl.pallas_call(
        _mla_proj_kernel,
        grid=(batch, tiles),
        in_specs=[
            pl.BlockSpec((TILE, D_MODEL), lambda b, t: (b * tiles + t, 0)),
            full(g), full(wa), full(qn), full(kvn), full(wuq), full(wukv), tab, tab, tab,
        ],
        out_specs=out_specs,
        out_shape=out_shape,
        compiler_params=pltpu.CompilerParams(
            dimension_semantics=("arbitrary", "arbitrary"), vmem_limit_bytes=VMEM_LIMIT),
        name="mla_proj",
    )(x, g, wa, qn, kvn, wuq, wukv, cos, slo, shi)


def _post_attn_kernel(h_ref, o_ref, wo_ref, g_ref, wr_ref, h1_ref, hn_ref, comb_ref, grow_ref):
    h1 = h_ref[...] + jnp.dot(o_ref[...], wo_ref[...], preferred_element_type=F32)
    h1_ref[...] = h1
    hn = _rms(h1, g_ref[...])
    hi = hn.astype(BF16)
    hn_ref[...] = hi
    lo = (hn - hi.astype(F32)).astype(BF16)
    w2 = wr_ref[...]
    r_hi = jnp.dot(hi, w2, preferred_element_type=F32)
    r_lo = jnp.dot(lo, w2, preferred_element_type=F32)
    logits = ((r_hi[:, :ROUTER_LANES] + r_hi[:, ROUTER_LANES:])
              + (r_lo[:, :ROUTER_LANES] + r_lo[:, ROUTER_LANES:]))
    lane_i = lax.broadcasted_iota(jnp.int32, logits.shape, 1)
    lane = lane_i.astype(F32)
    is_group = (lane_i >= N_EXPERTS) & (lane_i < N_EXPERTS + N_GROUPS)

    def first_max(mask):
        v = jnp.max(jnp.where(mask, logits, -jnp.inf), axis=1, keepdims=True)
        idx = jnp.min(jnp.where(mask & (logits == v), lane, 1e9), axis=1, keepdims=True)
        return v, idx

    g_max, g_lane = first_max(is_group)
    g_sum = jnp.sum(jnp.where(is_group, jnp.exp(logits - g_max), 0.0), axis=1, keepdims=True)
    g_w = 1.0 / g_sum
    group_of_lane = (lane_i >> 2).astype(F32)
    in_group = (lane_i < N_EXPERTS) & (group_of_lane == (g_lane - N_EXPERTS))
    v1, i1 = first_max(in_group)
    v2, i2 = first_max(in_group & (lane != i1))
    t = jnp.exp(v2 - v1)
    w1 = g_w / (1.0 + t)
    w2 = w1 * t
    comb_ref[...] = jnp.where(lane == i1, w1, 0.0) + jnp.where(lane == i2, w2, 0.0)
    picked = jnp.where(lane == g_lane, 1.0, 0.0).astype(BF16)
    srow = lax.broadcasted_iota(jnp.int32, (8, ROUTER_LANES), 0)
    slane = lax.broadcasted_iota(jnp.int32, (8, ROUTER_LANES), 1)
    is_g = (srow == 0) & (slane >= N_EXPERTS) & (slane < N_EXPERTS + N_GROUPS)
    sel = jnp.where(is_g, (slane - N_EXPERTS).astype(F32), 0.0).astype(BF16)
    grow_ref[0] = _nt_dot(sel, picked)


def _post_attn(h, o, wo, g, wr):
    n = h.shape[0]
    tile = POST_TILE
    row = lambda w: pl.BlockSpec((tile, w), lambda t: (t, 0))
    full = lambda a: pl.BlockSpec(a.shape, lambda t: (0,) * a.ndim)
    return pl.pallas_call(
        _post_attn_kernel,
        grid=(n // tile,),
        in_specs=[row(D_MODEL), row(D_MODEL), full(wo), full(g), full(wr)],
        out_specs=[row(D_MODEL), row(D_MODEL), row(ROUTER_LANES),
                   pl.BlockSpec((1, 8, tile), lambda t: (t, 0, 0))],
        out_shape=[
            jax.ShapeDtypeStruct((n, D_MODEL), F32),
            jax.ShapeDtypeStruct((n, D_MODEL), BF16),
            jax.ShapeDtypeStruct((n, ROUTER_LANES), F32),
            jax.ShapeDtypeStruct((n // tile, 8, tile), F32),
        ],
        compiler_params=pltpu.CompilerParams(
            dimension_semantics=("arbitrary",), vmem_limit_bytes=VMEM_LIMIT),
        name="post_attn",
    )(h, o, wo, g, wr)


def _moe_ple_kernel(hn_ref, comb_ref, grow_ref, h1_ref, p_ref, tri_ref, wg_ref, wu_ref, wd_ref,
                    pn_ref, pg_ref, pp_ref, fn_ref, out_ref, xs_ref, cs_ref, ys_ref, *, final):
    t = hn_ref.shape[0]
    gid = grow_ref[0, 0:1, :]
    g_iota = lax.broadcasted_iota(jnp.int32, (8, t), 0).astype(F32)
    member = gid == g_iota
    onehot = jnp.where(member, 1.0, 0.0)
    rank = jnp.dot(onehot.astype(BF16), tri_ref[...], preferred_element_type=F32)
    starts, rows = [], []
    start = jnp.int32(0)
    for g in range(N_GROUPS):
        count = jnp.sum(jnp.where(gid == float(g), 1, 0).astype(jnp.int32))
        padded = ((count + (SORT_ALIGN - 1)) >> 4) << 4
        starts.append(start)
        rows.append(padded)
        start = start + padded
    start_vec = jnp.zeros((8, t), F32)
    for g in range(1, N_GROUPS):
        start_vec = jnp.where(g_iota == float(g), starts[g].astype(F32), start_vec)
    dest = jnp.sum(jnp.where(member, rank + start_vec, 0.0), axis=0, keepdims=True)
    r_iota = lax.broadcasted_iota(jnp.int32, (SORT_ROWS, t), 0).astype(F32)
    perm = jnp.where(r_iota == dest, 1.0, 0.0).astype(BF16)

    comb = comb_ref[...]
    comb_hi = comb.astype(BF16).astype(F32)
    comb_lo = (comb - comb_hi).astype(BF16).astype(F32)
    packed = (comb_hi + pltpu.roll(comb_lo, N_EXPERTS, 1)).astype(BF16)
    both = jnp.dot(perm, jnp.concatenate([hn_ref[...], packed], axis=1),
                   preferred_element_type=F32)
    xs_ref[0:SORT_ROWS, :] = both[:, :D_MODEL].astype(BF16)
    cs_ref[0:SORT_ROWS, :] = both[:, D_MODEL:]
    xs_ref[SORT_ROWS:, :] = jnp.zeros((MOE_CHUNK, D_MODEL), BF16)
    cs_ref[SORT_ROWS:, :] = jnp.zeros((MOE_CHUNK, ROUTER_LANES), F32)
    ys_ref[...] = jnp.zeros(ys_ref.shape, BF16)

    lane = lax.broadcasted_iota(jnp.int32, (MOE_CHUNK, ROUTER_LANES), 1)
    for g in range(N_GROUPS):
        def chunk(j, _, g=g):
            off = pl.multiple_of(starts[g] + j * MOE_CHUNK, SORT_ALIGN)
            xc = xs_ref[pl.ds(off, MOE_CHUNK), :]
            cc = cs_ref[pl.ds(off, MOE_CHUNK), :]
            parts = []
            for e in range(g * PER_GROUP, (g + 1) * PER_GROUP):
                a = jnp.dot(xc, wg_ref[e], preferred_element_type=F32)
                u = jnp.dot(xc, wu_ref[e], preferred_element_type=F32)
                c = jnp.sum(jnp.where((lane == e) | (lane == e + N_EXPERTS), cc, 0.0),
                            axis=1, keepdims=True)
                parts.append((a * jax.nn.sigmoid(a) * u * c).astype(BF16))
            y = jnp.dot(jnp.concatenate(parts, axis=1), wd_ref[g], preferred_element_type=F32)
            ys_ref[pl.ds(off, MOE_CHUNK), :] = y.astype(BF16)
            return 0

        n_chunks = sum((rows[g] > k * MOE_CHUNK).astype(jnp.int32)
                       for k in range(SORT_ROWS // MOE_CHUNK))
        lax.fori_loop(0, n_chunks, chunk, 0)

    moe = lax.dot_general(perm, ys_ref[0:SORT_ROWS, :], (((0,), (0,)), ((), ())),
                          preferred_element_type=F32)
    h2 = h1_ref[...] + moe
    gate = jax.nn.sigmoid(
        jnp.dot(_rms(h2, pn_ref[...]).astype(BF16), pg_ref[...], preferred_element_type=F32))
    proj = jnp.dot(p_ref[...].astype(BF16), pp_ref[...], preferred_element_type=F32)
    h3 = h2 + gate * proj
    out_ref[...] = _rms(h3, fn_ref[...]) if final else h3


def _moe_ple(hn, comb, grow, h1, p, wg, wu, wd, pn, pg, pp, fn, final):
    n = hn.shape[0]
    tile = POST_TILE
    row = lambda w: pl.BlockSpec((tile, w), lambda t: (t, 0))
    once = lambda a: pl.BlockSpec(a.shape, lambda t: (0,) * a.ndim, pipeline_mode=pl.Buffered(1))
    tri = jnp.asarray(np.triu(np.ones((tile, tile), np.float32), 1), BF16)
    return pl.pallas_call(
        functools.partial(_moe_ple_kernel, final=final),
        grid=(n // tile,),
        in_specs=[row(D_MODEL), row(ROUTER_LANES), pl.BlockSpec((1, 8, tile), lambda t: (t, 0, 0)),
                  row(D_MODEL), row(PLE_DIM), once(tri), once(wg), once(wu), once(wd),
                  once(pn), once(pg), once(pp), once(fn)],
        out_specs=row(D_MODEL),
        out_shape=jax.ShapeDtypeStruct((n, D_MODEL), F32),
        scratch_shapes=[pltpu.VMEM((SORT_ROWS + MOE_CHUNK, D_MODEL), BF16),
                        pltpu.VMEM((SORT_ROWS + MOE_CHUNK, ROUTER_LANES), F32),
                        pltpu.VMEM((SORT_ROWS + MOE_CHUNK, D_MODEL), BF16)],
        compiler_params=pltpu.CompilerParams(
            dimension_semantics=("arbitrary",), vmem_limit_bytes=VMEM_LIMIT),
        name="moe_ple",
    )(hn, comb, grow, h1, p, tri, wg, wu, wd, pn, pg, pp, fn)


def _rope_tables(seq, half):
    inv = THETA ** (-np.arange(half, dtype=np.float64) / half)
    ang = np.arange(seq, dtype=np.float64)[:, None] * inv[None, :]
    return np.cos(ang).astype(np.float32), np.sin(ang).astype(np.float32)


def _split_bf16(w):
    hi = w.astype(BF16)
    return hi, (w - hi.astype(F32)).astype(BF16)


def kernel(x, p, attn_norm, ffn_norm, ple_norm, final_norm, moba_wqkv, moba_wo, mla_wdq, mla_qnorm,
           mla_wuq, mla_wdkv, mla_kvnorm, mla_wukv, mla_wo, moe_wgroup, moe_wexpert, moe_wgate,
           moe_wup, moe_wdown, ple_gate, ple_proj):
    batch, seq, d = x.shape
    n = batch * seq
    h = x.reshape(n, d)

    cos64, sin64 = _rope_tables(seq, HEAD_DIM // 2)
    moba_cos = jnp.asarray(np.concatenate([cos64, cos64], axis=1))
    moba_sin = jnp.asarray(np.concatenate([-sin64, sin64], axis=1))
    cos32, sin32 = _rope_tables(seq, MLA_ROPE // 2)
    z32 = np.zeros_like(sin32)
    mla_cos = jnp.asarray(np.concatenate([cos32] * 4, axis=1))
    mla_slo = jnp.asarray(np.concatenate([-sin32, z32, -sin32, z32], axis=1))
    mla_shi = jnp.asarray(np.concatenate([z32, sin32, z32, sin32], axis=1))

    row = lambda v: v.reshape(1, -1)

    for layer in range(N_LAYERS):
        j = layer // 2
        if layer % 2 == 0:
            q, k, vt = _moba_proj(h, row(attn_norm[layer]), moba_wqkv[j].astype(BF16),
                                  moba_cos, moba_sin, batch, seq)
            o = _moba_attn(q, k, vt)
            wo = moba_wo[j]
        else:
            wuq = mla_wuq[j].reshape(MLA_Q_RANK, HEADS, MLA_NOPE + MLA_ROPE)
            wuq = jnp.concatenate([wuq[:, :, :MLA_NOPE].reshape(MLA_Q_RANK, -1),
                                   wuq[:, :, MLA_NOPE:].reshape(MLA_Q_RANK, -1)], axis=1)
            wukv = mla_wukv[j].reshape(MLA_KV_RANK, HEADS, MLA_NOPE + HEAD_DIM)
            wukv = jnp.concatenate([wukv[:, :, :MLA_NOPE].reshape(MLA_KV_RANK, -1),
                                    wukv[:, :, MLA_NOPE:].reshape(MLA_KV_RANK, -1)], axis=1)
            wa = jnp.concatenate(
                [mla_wdq[j], mla_wdkv[j], jnp.zeros((d, 128 - MLA_ROPE), F32)], axis=1)
            q, k, vt = _mla_proj(h, row(attn_norm[layer]), wa.astype(BF16), row(mla_qnorm[j]),
                                 row(mla_kvnorm[j]), wuq.astype(BF16), wukv.astype(BF16),
                                 mla_cos, mla_slo, mla_shi, batch, seq)
            o = _causal_attn(q, k, vt)
            wo = mla_wo[j]

        wr = jnp.concatenate(
            [moe_wexpert[layer], moe_wgroup[layer],
             jnp.zeros((d, ROUTER_LANES - N_EXPERTS - N_GROUPS), F32)], axis=1)
        h1, hn, comb, grow = _post_attn(h, o.reshape(n, -1), wo.astype(BF16), row(ffn_norm[layer]),
                                        jnp.concatenate(_split_bf16(wr), axis=1))

        h = _moe_ple(hn, comb, grow, h1, p[layer].reshape(n, PLE_DIM), moe_wgate[layer].astype(BF16),
                     moe_wup[layer].astype(BF16),
                     moe_wdown[layer].reshape(N_GROUPS, PER_GROUP * EXPERT_FF, d).astype(BF16),
                     row(ple_norm[layer]), ple_gate[layer].astype(BF16),
                     ple_proj[layer].astype(BF16), row(final_norm), layer == N_LAYERS - 1)

    return h.reshape(batch, seq, d)
```

```python
import functools

import numpy as np
import jax
import jax.numpy as jnp
from jax import lax
from jax.experimental import pallas as pl
from jax.experimental.pallas import tpu as pltpu

F32 = jnp.float32
BF16 = jnp.bfloat16

D_MODEL = 1024
N_LAYERS = 2
EPS = 1e-6
THETA = 10000.0
NEG = -1e30
LOG2E = 1.4426950408889634

HEADS = 8
HEAD_DIM = 128
MOBA_BLOCK = 256
MOBA_TOPK = 3

MLA_Q_RANK = 384
MLA_KV_RANK = 256
MLA_NOPE = 128
MLA_ROPE = 64
MLA_QK_PAD = 256

N_GROUPS = 4
PER_GROUP = 4
N_EXPERTS = 16
EXPERT_FF = 256
ROUTER_LANES = 128

PLE_DIM = 256

TILE = 512
Q_TILE = 1024
V_ROWS = HEAD_DIM + 16
ATTN_HEADS = 2
POST_TILE = 512
SORT_ALIGN = 16
SORT_ROWS = 640
MOE_CHUNK = 144

VMEM_LIMIT = 56 * 1024 * 1024


def _rms(x, g):
    return x * lax.rsqrt(jnp.mean(x * x, axis=-1, keepdims=True) + EPS) * g


def _nt_dot(a, b):
    return lax.dot_general(a, b, (((1,), (1,)), ((), ())), preferred_element_type=F32)


def _moba_proj_kernel(x_ref, g_ref, w_ref, cos_ref, sin_ref, qt_ref, k_ref, vt_ref):
    hn = _rms(x_ref[...], g_ref[...])
    qkv = jnp.dot(hn.astype(BF16), w_ref[...], preferred_element_type=F32)
    cos = cos_ref[...]
    sin = sin_ref[...]
    scale = HEAD_DIM ** -0.5 * LOG2E
    hd = HEADS * HEAD_DIM
    row = lax.broadcasted_iota(jnp.int32, (TILE, HEAD_DIM), 0)
    lane = lax.broadcasted_iota(jnp.int32, (TILE, HEAD_DIM), 1)
    block = pl.program_id(1) * (TILE // MOBA_BLOCK) + (row >> 8)
    block_onehot = jnp.where(lane == block, 1.0, 0.0).astype(BF16)
    for h in range(HEADS):
        lo = h * HEAD_DIM
        qh = qkv[:, lo:lo + HEAD_DIM]
        qh = qh * cos + pltpu.roll(qh, HEAD_DIM // 2, 1) * sin
        qt_ref[0, h] = (qh * scale).T.astype(BF16)
        kh = qkv[:, hd + lo:hd + lo + HEAD_DIM]
        kh = (kh * cos + pltpu.roll(kh, HEAD_DIM // 2, 1) * sin).astype(BF16)
        vh = qkv[:, 2 * hd + lo:2 * hd + lo + HEAD_DIM]
        _store_kv(k_ref, vt_ref, h, jnp.concatenate([kh, block_onehot], axis=1), vh)


def _store_kv(k_ref, vt_ref, h, kh, vh):
    k_ref[0, h, 0] = kh
    vt_ref[0, h, 0] = jnp.concatenate(
        [vh.T, jnp.ones((V_ROWS - HEAD_DIM, TILE), F32)], axis=0).astype(BF16)


def _qkv_out(batch, seq, dq, dk):
    steps = seq // TILE
    specs = [
        pl.BlockSpec((1, HEADS, dq, TILE), lambda b, t: (b, 0, 0, t)),
        pl.BlockSpec((1, HEADS, 1, TILE, dk), lambda b, t: (b, 0, t, 0, 0)),
        pl.BlockSpec((1, HEADS, 1, V_ROWS, TILE), lambda b, t: (b, 0, t, 0, 0)),
    ]
    shapes = [
        jax.ShapeDtypeStruct((batch, HEADS, dq, seq), BF16),
        jax.ShapeDtypeStruct((batch, HEADS, steps, TILE, dk), BF16),
        jax.ShapeDtypeStruct((batch, HEADS, steps, V_ROWS, TILE), BF16),
    ]
    return specs, shapes


def _moba_proj(x, g, w, cos2, sin2, batch, seq):
    tiles = seq // TILE
    out_specs, out_shape = _qkv_out(batch, seq, HEAD_DIM, 2 * HEAD_DIM)
    return pl.pallas_call(
        _moba_proj_kernel,
        grid=(batch, tiles),
        in_specs=[
            pl.BlockSpec((TILE, D_MODEL), lambda b, t: (b * tiles + t, 0)),
            pl.BlockSpec((1, D_MODEL), lambda b, t: (0, 0)),
            pl.BlockSpec((D_MODEL, 3 * HEADS * HEAD_DIM), lambda b, t: (0, 0)),
            pl.BlockSpec((TILE, HEAD_DIM), lambda b, t: (t, 0)),
            pl.BlockSpec((TILE, HEAD_DIM), lambda b, t: (t, 0)),
        ],
        out_specs=out_specs,
        out_shape=out_shape,
        compiler_params=pltpu.CompilerParams(
            dimension_semantics=("arbitrary", "arbitrary"), vmem_limit_bytes=VMEM_LIMIT),
        name="moba_proj",
    )(x, g, w, cos2, sin2)


def _attn_scratch(nh):
    return ([pltpu.VMEM((TILE, Q_TILE), F32)] * (2 * nh) + [pltpu.VMEM((1, Q_TILE), F32)] * (2 * nh)
            + [pltpu.VMEM((1, Q_TILE), F32)] * nh + [pltpu.VMEM((V_ROWS, Q_TILE), F32)] * nh)


def _attention(q_cols, q_next, k_ref, vt_ref, o_ref, scratch, i):
    nh, tq = k_ref.shape[1], o_ref.shape[1]
    assert tq == 2 * TILE
    s_refs = (scratch[:nh], scratch[nh:2 * nh])
    smax_refs = (scratch[2 * nh:3 * nh], scratch[3 * nh:4 * nh])
    m_refs = scratch[4 * nh:5 * nh]
    acc_refs = scratch[5 * nh:6 * nh]
    everything, left, right = slice(0, tq), slice(0, TILE), slice(TILE, tq)

    def produce(h, step, par, cols=everything, q=q_cols):
        s = jnp.dot(k_ref[0, h, step], q(h, cols), preferred_element_type=F32)
        s_refs[par][h][:, cols] = s
        smax_refs[par][h][:, cols] = jnp.max(s, axis=0, keepdims=True)

    def consume(h, step, par, cols=everything, causal=False):
        s = s_refs[par][h][:, cols]
        if causal:
            kpos = lax.broadcasted_iota(jnp.int32, s.shape, 0)
            qpos = lax.broadcasted_iota(jnp.int32, s.shape, 1)
            s = jnp.where(kpos <= qpos, s, NEG)
            s_max = jnp.max(s, axis=0, keepdims=True)
        else:
            s_max = smax_refs[par][h][:, cols]
        m = m_refs[h][:, cols]
        m_new = jnp.maximum(m, s_max)
        m_refs[h][:, cols] = m_new
        p = jnp.exp2(s - m_new).astype(BF16)
        pv = jnp.dot(vt_ref[0, h, step], p, preferred_element_type=F32)
        acc_refs[h][:, cols] = jnp.exp2(m - m_new) * acc_refs[h][:, cols] + pv

    for h in range(nh):
        m_refs[h][...] = jnp.full((1, tq), NEG, F32)
        acc_refs[h][...] = jnp.zeros((V_ROWS, tq), F32)

    @pl.when(i == 0)
    def _():
        for h in range(nh):
            produce(h, 0, 0)

    def run(first, count):
        for c in range(count):
            for h in range(nh):
                produce(h, first + c + 1, 1 - c % 2)
                consume(h, first + c, c % 2)

    def body4(u, _):
        run(4 * u, 4)
        return 0

    def body2(u, _):
        run(4 * (i >> 1), 2)
        return 0

    lax.fori_loop(0, i >> 1, body4, 0)
    lax.fori_loop(0, i & 1, body2, 0)
    for h in range(nh):
        produce(h, 2 * i + 1, 1, right)
        consume(h, 2 * i, 0, left, causal=True)
        consume(h, 2 * i, 0, right)
    for h in range(nh):
        produce(h, 0, 0, q=q_next)
        consume(h, 2 * i + 1, 1, right, causal=True)

    for h in range(nh):
        acc = acc_refs[h][...]
        out = acc[:HEAD_DIM] / acc[HEAD_DIM:HEAD_DIM + 1]
        o_ref[0, :, h * HEAD_DIM:(h + 1) * HEAD_DIM] = out.T.astype(BF16)


def _moba_attn_kernel(qt_ref, qtn_ref, k_ref, vt_ref, o_ref, kmean_ref, qa_ref, *scratch):
    i = pl.program_id(2)
    nh, _, tq = qt_ref.shape[1:]
    steps = k_ref.shape[2]
    nb = TILE // MOBA_BLOCK
    nblk = steps * nb

    nidx = lax.broadcasted_iota(jnp.int32, (nblk, tq), 0)

    def build(h, qt, tile, slot):
        own = (tile * tq + lax.broadcasted_iota(jnp.int32, (nblk, tq), 1)) >> 8
        past = nidx < own
        km = kmean_ref[h]
        km_hi = km.astype(BF16)
        km_lo = (km - km_hi.astype(F32)).astype(BF16)
        gate = (jnp.dot(km_hi, qt, preferred_element_type=F32)
                + jnp.dot(km_lo, qt, preferred_element_type=F32))
        g1 = jnp.where(past, gate, -jnp.inf)
        thr = jnp.max(g1, axis=0, keepdims=True)
        g = g1
        for _ in range(MOBA_TOPK - 1):
            g = jnp.where(g >= thr, -jnp.inf, g)
            thr = jnp.max(g, axis=0, keepdims=True)
        keep = (past & (g1 >= thr)) | (nidx == own)
        qa_ref[slot, h, :HEAD_DIM, :] = qt
        qa_ref[slot, h, HEAD_DIM:HEAD_DIM + nblk, :] = jnp.where(keep, 0.0, NEG).astype(BF16)
        qa_ref[slot, h, HEAD_DIM + nblk:, :] = jnp.zeros((HEAD_DIM - nblk, tq), BF16)

    @pl.when(i == 0)
    def _():
        for h in range(nh):
            for n in range(nblk):
                kb = k_ref[0, h, n // nb, (n % nb) * MOBA_BLOCK:(n % nb + 1) * MOBA_BLOCK, :HEAD_DIM]
                kmean_ref[h, n:n + 1, :] = jnp.mean(kb.astype(F32), axis=0, keepdims=True)
        for h in range(nh):
            build(h, qt_ref[0, h], 0, 0)

    cur, nxt = i & 1, (i + 1) & 1
    for h in range(nh):
        build(h, qtn_ref[0, h], i + 1, nxt)

    _attention(lambda h, cols: qa_ref[cur, h, :, cols], lambda h, cols: qa_ref[nxt, h, :, cols],
               k_ref, vt_ref, o_ref, scratch, i)


def _causal_attn_kernel(qt_ref, qtn_ref, k_ref, vt_ref, o_ref, *scratch):
    _attention(lambda h, cols: qt_ref[0, h, :, cols], lambda h, cols: qtn_ref[0, h, :, cols],
               k_ref, vt_ref, o_ref, scratch, pl.program_id(2))


def _attn_call(kernel_fn, name, qt, k, vt, scratch):
    batch, heads, dq, seq = qt.shape
    dk = k.shape[-1]
    steps = seq // TILE
    n_q = seq // Q_TILE
    nh = ATTN_HEADS
    return pl.pallas_call(
        kernel_fn,
        grid=(batch, heads // nh, n_q),
        in_specs=[
            pl.BlockSpec((1, nh, dq, Q_TILE), lambda b, h, i: (b, h, 0, i)),
            pl.BlockSpec((1, nh, dq, Q_TILE), lambda b, h, i: (b, h, 0, jnp.minimum(i + 1, n_q - 1))),
            pl.BlockSpec((1, nh, steps, TILE, dk), lambda b, h, i: (b, h, 0, 0, 0)),
            pl.BlockSpec((1, nh, steps, V_ROWS, TILE), lambda b, h, i: (b, h, 0, 0, 0)),
        ],
        out_specs=pl.BlockSpec((1, Q_TILE, nh * HEAD_DIM), lambda b, h, i: (b, i, h)),
        out_shape=jax.ShapeDtypeStruct((batch, seq, heads * HEAD_DIM), BF16),
        scratch_shapes=scratch + _attn_scratch(nh),
        compiler_params=pltpu.CompilerParams(
            dimension_semantics=("arbitrary", "arbitrary", "arbitrary"),
            vmem_limit_bytes=VMEM_LIMIT),
        name=name,
    )(qt, qt, k, vt)


def _moba_attn(qt, k, vt):
    seq = qt.shape[3]
    scratch = [pltpu.VMEM((ATTN_HEADS, seq // MOBA_BLOCK, HEAD_DIM), F32),
               pltpu.VMEM((2, ATTN_HEADS, 2 * HEAD_DIM, Q_TILE), BF16)]
    return _attn_call(_moba_attn_kernel, "moba_attn", qt, k, vt, scratch)


def _causal_attn(qt, k, vt):
    return _attn_call(_causal_attn_kernel, "mla_attn", qt, k, vt, [])


def _rope_pairs32(x, cos, sin_lo, sin_hi):
    n = x.shape[1]
    return x * cos + pltpu.roll(x, n - 32, 1) * sin_lo + pltpu.roll(x, 32, 1) * sin_hi


def _mla_proj_kernel(x_ref, g_ref, wa_ref, qn_ref, kvn_ref, wuq_ref, wukv_ref,
                     cos_ref, slo_ref, shi_ref, qt_ref, k_ref, vt_ref):
    hn = _rms(x_ref[...], g_ref[...])
    r = jnp.dot(hn.astype(BF16), wa_ref[...], preferred_element_type=F32)
    cq = _rms(r[:, :MLA_Q_RANK], qn_ref[...])
    ckv = _rms(r[:, MLA_Q_RANK:MLA_Q_RANK + MLA_KV_RANK], kvn_ref[...])
    kr = r[:, MLA_Q_RANK + MLA_KV_RANK:]
    q = jnp.dot(cq.astype(BF16), wuq_ref[...], preferred_element_type=F32)
    kv = jnp.dot(ckv.astype(BF16), wukv_ref[...], preferred_element_type=F32)
    cos = cos_ref[...]
    slo = slo_ref[...]
    shi = shi_ref[...]
    nn = HEADS * MLA_NOPE
    reps = HEADS * MLA_ROPE // 128
    qr = _rope_pairs32(q[:, nn:], jnp.concatenate([cos] * reps, axis=1),
                       jnp.concatenate([slo] * reps, axis=1), jnp.concatenate([shi] * reps, axis=1))
    kr = _rope_pairs32(kr, cos, slo, shi)
    scale = (MLA_NOPE + MLA_ROPE) ** -0.5 * LOG2E
    low_half = lax.broadcasted_iota(jnp.int32, kr.shape, 1) < MLA_ROPE
    for h in range(HEADS):
        pair = qr[:, (h // 2) * 128:(h // 2 + 1) * 128]
        if h % 2:
            pair = pltpu.roll(pair, MLA_ROPE, 1)
        qh = jnp.concatenate(
            [q[:, h * MLA_NOPE:(h + 1) * MLA_NOPE], jnp.where(low_half, pair, 0.0)], axis=1)
        qt_ref[0, h] = (qh * scale).T.astype(BF16)
        kh = jnp.concatenate([kv[:, h * MLA_NOPE:(h + 1) * MLA_NOPE], kr], axis=1).astype(BF16)
        vh = kv[:, nn + h * HEAD_DIM:nn + (h + 1) * HEAD_DIM]
        _store_kv(k_ref, vt_ref, h, kh, vh)


def _mla_proj(x, g, wa, qn, kvn, wuq, wukv, cos, slo, shi, batch, seq):
    tiles = seq // TILE
    full = lambda a: pl.BlockSpec(a.shape, lambda b, t: (0,) * a.ndim)
    tab = pl.BlockSpec((TILE, 128), lambda b, t: (t, 0))
    out_specs, out_shape = _qkv_out(batch, seq, MLA_QK_PAD, MLA_QK_PAD)
    return pl.pallas_call(
        _mla_proj_kernel,
        grid=(batch, tiles),
        in_specs=[
            pl.BlockSpec((TILE, D_MODEL), lambda b, t: (b * tiles + t, 0)),
            full(g), full(wa), full(qn), full(kvn), full(wuq), full(wukv), tab, tab, tab,
        ],
        out_specs=out_specs,
        out_shape=out_shape,
        compiler_params=pltpu.CompilerParams(
            dimension_semantics=("arbitrary", "arbitrary"), vmem_limit_bytes=VMEM_LIMIT),
        name="mla_proj",
    )(x, g, wa, qn, kvn, wuq, wukv, cos, slo, shi)


def _post_attn_kernel(h_ref, o_ref, wo_ref, g_ref, wr_ref, h1_ref, hn_ref, comb_ref, grow_ref):
    h1 = h_ref[...] + jnp.dot(o_ref[...], wo_ref[...], preferred_element_type=F32)
    h1_ref[...] = h1
    hn = _rms(h1, g_ref[...])
    hi = hn.astype(BF16)
    hn_ref[...] = hi
    lo = (hn - hi.astype(F32)).astype(BF16)
    w2 = wr_ref[...]
    r_hi = jnp.dot(hi, w2, preferred_element_type=F32)
    r_lo = jnp.dot(lo, w2, preferred_element_type=F32)
    logits = ((r_hi[:, :ROUTER_LANES] + r_hi[:, ROUTER_LANES:])
              + (r_lo[:, :ROUTER_LANES] + r_lo[:, ROUTER_LANES:]))
    lane_i = lax.broadcasted_iota(jnp.int32, logits.shape, 1)
    lane = lane_i.astype(F32)
    is_group = (lane_i >= N_EXPERTS) & (lane_i < N_EXPERTS + N_GROUPS)

    def first_max(mask):
        v = jnp.max(jnp.where(mask, logits, -jnp.inf), axis=1, keepdims=True)
        idx = jnp.min(jnp.where(mask & (logits == v), lane, 1e9), axis=1, keepdims=True)
        return v, idx

    g_max, g_lane = first_max(is_group)
    g_sum = jnp.sum(jnp.where(is_group, jnp.exp(logits - g_max), 0.0), axis=1, keepdims=True)
    g_w = 1.0 / g_sum
    group_of_lane = (lane_i >> 2).astype(F32)
    in_group = (lane_i < N_EXPERTS) & (group_of_lane == (g_lane - N_EXPERTS))
    v1, i1 = first_max(in_group)
    v2, i2 = first_max(in_group & (lane != i1))
    t = jnp.exp(v2 - v1)
    w1 = g_w / (1.0 + t)
    w2 = w1 * t
    comb_ref[...] = jnp.where(lane == i1, w1, 0.0) + jnp.where(lane == i2, w2, 0.0)
    picked = jnp.where(lane == g_lane, 1.0, 0.0).astype(BF16)
    srow = lax.broadcasted_iota(jnp.int32, (8, ROUTER_LANES), 0)
    slane = lax.broadcasted_iota(jnp.int32, (8, ROUTER_LANES), 1)
    is_g = (srow == 0) & (slane >= N_EXPERTS) & (slane < N_EXPERTS + N_GROUPS)
    sel = jnp.where(is_g, (slane - N_EXPERTS).astype(F32), 0.0).astype(BF16)
    grow_ref[0] = _nt_dot(sel, picked)


def _post_attn(h, o, wo, g, wr):
    n = h.shape[0]
    tile = POST_TILE
    row = lambda w: pl.BlockSpec((tile, w), lambda t: (t, 0))
    full = lambda a: pl.BlockSpec(a.shape, lambda t: (0,) * a.ndim)
    return pl.pallas_call(
        _post_attn_kernel,
        grid=(n // tile,),
        in_specs=[row(D_MODEL), row(D_MODEL), full(wo), full(g), full(wr)],
        out_specs=[row(D_MODEL), row(D_MODEL), row(ROUTER_LANES),
                   pl.BlockSpec((1, 8, tile), lambda t: (t, 0, 0))],
        out_shape=[
            jax.ShapeDtypeStruct((n, D_MODEL), F32),
            jax.ShapeDtypeStruct((n, D_MODEL), BF16),
            jax.ShapeDtypeStruct((n, ROUTER_LANES), F32),
            jax.ShapeDtypeStruct((n // tile, 8, tile), F32),
        ],
        compiler_params=pltpu.CompilerParams(
            dimension_semantics=("arbitrary",), vmem_limit_bytes=VMEM_LIMIT),
        name="post_attn",
    )(h, o, wo, g, wr)


def _moe_ple_kernel(hn_ref, comb_ref, grow_ref, h1_ref, p_ref, tri_ref, wg_ref, wu_ref, wd_ref,
                    pn_ref, pg_ref, pp_ref, fn_ref, out_ref, xs_ref, cs_ref, ys_ref, *, final):
    t = hn_ref.shape[0]
    gid = grow_ref[0, 0:1, :]
    g_iota = lax.broadcasted_iota(jnp.int32, (8, t), 0).astype(F32)
    member = gid == g_iota
    onehot = jnp.where(member, 1.0, 0.0)
    rank = jnp.dot(onehot.astype(BF16), tri_ref[...], preferred_element_type=F32)
    starts, rows = [], []
    start = jnp.int32(0)
    for g in range(N_GROUPS):
        count = jnp.sum(jnp.where(gid == float(g), 1, 0).astype(jnp.int32))
        padded = ((count + (SORT_ALIGN - 1)) >> 4) << 4
        starts.append(start)
        rows.append(padded)
        start = start + padded
    start_vec = jnp.zeros((8, t), F32)
    for g in range(1, N_GROUPS):
        start_vec = jnp.where(g_iota == float(g), starts[g].astype(F32), start_vec)
    dest = jnp.sum(jnp.where(member, rank + start_vec, 0.0), axis=0, keepdims=True)
    r_iota = lax.broadcasted_iota(jnp.int32, (SORT_ROWS, t), 0).astype(F32)
    perm = jnp.where(r_iota == dest, 1.0, 0.0).astype(BF16)

    comb = comb_ref[...]
    comb_hi = comb.astype(BF16).astype(F32)
    comb_lo = (comb - comb_hi).astype(BF16).astype(F32)
    packed = (comb_hi + pltpu.roll(comb_lo, N_EXPERTS, 1)).astype(BF16)
    both = jnp.dot(perm, jnp.concatenate([hn_ref[...], packed], axis=1),
                   preferred_element_type=F32)
    xs_ref[0:SORT_ROWS, :] = both[:, :D_MODEL].astype(BF16)
    cs_ref[0:SORT_ROWS, :] = both[:, D_MODEL:]
    xs_ref[SORT_ROWS:, :] = jnp.zeros((MOE_CHUNK, D_MODEL), BF16)
    cs_ref[SORT_ROWS:, :] = jnp.zeros((MOE_CHUNK, ROUTER_LANES), F32)
    ys_ref[...] = jnp.zeros(ys_ref.shape, BF16)

    lane = lax.broadcasted_iota(jnp.int32, (MOE_CHUNK, ROUTER_LANES), 1)

    def chunk(g, j):
        off = pl.multiple_of(starts[g] + j * MOE_CHUNK, SORT_ALIGN)
        xc = xs_ref[pl.ds(off, MOE_CHUNK), :]
        cc = cs_ref[pl.ds(off, MOE_CHUNK), :]
        parts = []
        for e in range(g * PER_GROUP, (g + 1) * PER_GROUP):
            a = jnp.dot(xc, wg_ref[e], preferred_element_type=F32)
            u = jnp.dot(xc, wu_ref[e], preferred_element_type=F32)
            c = jnp.sum(jnp.where((lane == e) | (lane == e + N_EXPERTS), cc, 0.0),
                        axis=1, keepdims=True)
            parts.append((a * jax.nn.sigmoid(a) * u * c).astype(BF16))
        y = jnp.dot(jnp.concatenate(parts, axis=1), wd_ref[g], preferred_element_type=F32)
        ys_ref[pl.ds(off, MOE_CHUNK), :] += y.astype(BF16)

    for g in range(N_GROUPS):
        chunk(g, 0)
    for g in range(N_GROUPS):
        n_chunks = sum((rows[g] > k * MOE_CHUNK).astype(jnp.int32)
                       for k in range(SORT_ROWS // MOE_CHUNK))

        def rest(j, _, g=g):
            chunk(g, j)
            return 0

        lax.fori_loop(1, jnp.maximum(n_chunks, 1), rest, 0)

    moe = lax.dot_general(perm, ys_ref[0:SORT_ROWS, :], (((0,), (0,)), ((), ())),
                          preferred_element_type=F32)
    h2 = h1_ref[...] + moe
    gate = jax.nn.sigmoid(
        jnp.dot(_rms(h2, pn_ref[...]).astype(BF16), pg_ref[...], preferred_element_type=F32))
    proj = jnp.dot(p_ref[...].astype(BF16), pp_ref[...], preferred_element_type=F32)
    h3 = h2 + gate * proj
    out_ref[...] = _rms(h3, fn_ref[...]) if final else h3


def _moe_ple(hn, comb, grow, h1, p, wg, wu, wd, pn, pg, pp, fn, final):
    n = hn.shape[0]
    tile = POST_TILE
    row = lambda w: pl.BlockSpec((tile, w), lambda t: (t, 0))
    once = lambda a: pl.BlockSpec(a.shape, lambda t: (0,) * a.ndim, pipeline_mode=pl.Buffered(1))
    tri = jnp.asarray(np.triu(np.ones((tile, tile), np.float32), 1), BF16)
    return pl.pallas_call(
        functools.partial(_moe_ple_kernel, final=final),
        grid=(n // tile,),
        in_specs=[row(D_MODEL), row(ROUTER_LANES), pl.BlockSpec((1, 8, tile), lambda t: (t, 0, 0)),
                  row(D_MODEL), row(PLE_DIM), once(tri), once(wg), once(wu), once(wd),
                  once(pn), once(pg), once(pp), once(fn)],
        out_specs=row(D_MODEL),
        out_shape=jax.ShapeDtypeStruct((n, D_MODEL), F32),
        scratch_shapes=[pltpu.VMEM((SORT_ROWS + MOE_CHUNK, D_MODEL), BF16),
                        pltpu.VMEM((SORT_ROWS + MOE_CHUNK, ROUTER_LANES), F32),
                        pltpu.VMEM((SORT_ROWS + MOE_CHUNK, D_MODEL), BF16)],
        compiler_params=pltpu.CompilerParams(
            dimension_semantics=("arbitrary",), vmem_limit_bytes=VMEM_LIMIT),
        name="moe_ple",
    )(hn, comb, grow, h1, p, tri, wg, wu, wd, pn, pg, pp, fn)


def _rope_tables(seq, half):
    inv = THETA ** (-np.arange(half, dtype=np.float64) / half)
    ang = np.arange(seq, dtype=np.float64)[:, None] * inv[None, :]
    return np.cos(ang).astype(np.float32), np.sin(ang).astype(np.float32)


def _split_bf16(w):
    hi = w.astype(BF16)
    return hi, (w - hi.astype(F32)).astype(BF16)


def kernel(x, p, attn_norm, ffn_norm, ple_norm, final_norm, moba_wqkv, moba_wo, mla_wdq, mla_qnorm,
           mla_wuq, mla_wdkv, mla_kvnorm, mla_wukv, mla_wo, moe_wgroup, moe_wexpert, moe_wgate,
           moe_wup, moe_wdown, ple_gate, ple_proj):
    batch, seq, d = x.shape
    n = batch * seq
    h = x.reshape(n, d)

    cos64, sin64 = _rope_tables(seq, HEAD_DIM // 2)
    moba_cos = jnp.asarray(np.concatenate([cos64, cos64], axis=1))
    moba_sin = jnp.asarray(np.concatenate([-sin64, sin64], axis=1))
    cos32, sin32 = _rope_tables(seq, MLA_ROPE // 2)
    z32 = np.zeros_like(sin32)
    mla_cos = jnp.asarray(np.concatenate([cos32] * 4, axis=1))
    mla_slo = jnp.asarray(np.concatenate([-sin32, z32, -sin32, z32], axis=1))
    mla_shi = jnp.asarray(np.concatenate([z32, sin32, z32, sin32], axis=1))

    row = lambda v: v.reshape(1, -1)

    for layer in range(N_LAYERS):
        j = layer // 2
        if layer % 2 == 0:
            q, k, vt = _moba_proj(h, row(attn_norm[layer]), moba_wqkv[j].astype(BF16),
                                  moba_cos, moba_sin, batch, seq)
            o = _moba_attn(q, k, vt)
            wo = moba_wo[j]
        else:
            wuq = mla_wuq[j].reshape(MLA_Q_RANK, HEADS, MLA_NOPE + MLA_ROPE)
            wuq = jnp.concatenate([wuq[:, :, :MLA_NOPE].reshape(MLA_Q_RANK, -1),
                                   wuq[:, :, MLA_NOPE:].reshape(MLA_Q_RANK, -1)], axis=1)
            wukv = mla_wukv[j].reshape(MLA_KV_RANK, HEADS, MLA_NOPE + HEAD_DIM)
            wukv = jnp.concatenate([wukv[:, :, :MLA_NOPE].reshape(MLA_KV_RANK, -1),
                                    wukv[:, :, MLA_NOPE:].reshape(MLA_KV_RANK, -1)], axis=1)
            wa = jnp.concatenate(
                [mla_wdq[j], mla_wdkv[j], jnp.zeros((d, 128 - MLA_ROPE), F32)], axis=1)
            q, k, vt = _mla_proj(h, row(attn_norm[layer]), wa.astype(BF16), row(mla_qnorm[j]),
                                 row(mla_kvnorm[j]), wuq.astype(BF16), wukv.astype(BF16),
                                 mla_cos, mla_slo, mla_shi, batch, seq)
            o = _causal_attn(q, k, vt)
            wo = mla_wo[j]

        wr = jnp.concatenate(
            [moe_wexpert[layer], moe_wgroup[layer],
             jnp.zeros((d, ROUTER_LANES - N_EXPERTS - N_GROUPS), F32)], axis=1)
        h1, hn, comb, grow = _post_attn(h, o.reshape(n, -1), wo.astype(BF16), row(ffn_norm[layer]),
                                        jnp.concatenate(_split_bf16(wr), axis=1))

        h = _moe_ple(hn, comb, grow, h1, p[layer].reshape(n, PLE_DIM), moe_wgate[layer].astype(BF16),
                     moe_wup[layer].astype(BF16),
                     moe_wdown[layer].reshape(N_GROUPS, PER_GROUP * EXPERT_FF, d).astype(BF16),
                     row(ple_norm[layer]), ple_gate[layer].astype(BF16),
                     ple_proj[layer].astype(BF16), row(final_norm), layer == N_LAYERS - 1)

    return h.reshape(batch, seq, d)
```

```python
import functools

import numpy as np
import jax
import jax.numpy as jnp
from jax import lax
from jax.experimental import pallas as pl
from jax.experimental.pallas import tpu as pltpu

F32 = jnp.float32
BF16 = jnp.bfloat16

D_MODEL = 1024
N_LAYERS = 2
EPS = 1e-6
THETA = 10000.0
NEG = -1e30
LOG2E = 1.4426950408889634

HEADS = 8
HEAD_DIM = 128
MOBA_BLOCK = 256
MOBA_TOPK = 3

MLA_Q_RANK = 384
MLA_KV_RANK = 256
MLA_NOPE = 128
MLA_ROPE = 64
MLA_QK_PAD = 256

N_GROUPS = 4
PER_GROUP = 4
N_EXPERTS = 16
EXPERT_FF = 256
ROUTER_LANES = 128

PLE_DIM = 256

TILE = 512
Q_TILE = 1024
V_ROWS = HEAD_DIM + 16
ATTN_HEADS = 2
POST_TILE = 512
POST_ATTN_TILE = 1024
SORT_ALIGN = 16
SORT_ROWS = 576
MOE_CHUNK = 144

VMEM_LIMIT = 56 * 1024 * 1024


def _rms(x, g):
    return x * lax.rsqrt(jnp.mean(x * x, axis=-1, keepdims=True) + EPS) * g


def _nt_dot(a, b):
    return lax.dot_general(a, b, (((1,), (1,)), ((), ())), preferred_element_type=F32)


def _moba_proj_kernel(x_ref, g_ref, w_ref, cos_ref, sin_ref, qt_ref, k_ref, vt_ref):
    hn = _rms(x_ref[...], g_ref[...])
    qkv = jnp.dot(hn.astype(BF16), w_ref[...], preferred_element_type=F32)
    cos = cos_ref[...]
    sin = sin_ref[...]
    scale = HEAD_DIM ** -0.5 * LOG2E
    hd = HEADS * HEAD_DIM
    row = lax.broadcasted_iota(jnp.int32, (TILE, HEAD_DIM), 0)
    lane = lax.broadcasted_iota(jnp.int32, (TILE, HEAD_DIM), 1)
    block = pl.program_id(1) * (TILE // MOBA_BLOCK) + (row >> 8)
    block_onehot = jnp.where(lane == block, 1.0, 0.0).astype(BF16)
    for h in range(HEADS):
        lo = h * HEAD_DIM
        qh = qkv[:, lo:lo + HEAD_DIM]
        qh = qh * cos + pltpu.roll(qh, HEAD_DIM // 2, 1) * sin
        qt_ref[0, h] = (qh * scale).T.astype(BF16)
        kh = qkv[:, hd + lo:hd + lo + HEAD_DIM]
        kh = (kh * cos + pltpu.roll(kh, HEAD_DIM // 2, 1) * sin).astype(BF16)
        vh = qkv[:, 2 * hd + lo:2 * hd + lo + HEAD_DIM]
        _store_kv(k_ref, vt_ref, h, jnp.concatenate([kh, block_onehot], axis=1), vh)


def _store_kv(k_ref, vt_ref, h, kh, vh):
    k_ref[0, h, 0] = kh
    vt_ref[0, h, 0] = jnp.concatenate(
        [vh.T, jnp.ones((V_ROWS - HEAD_DIM, TILE), F32)], axis=0).astype(BF16)


def _qkv_out(batch, seq, dq, dk):
    steps = seq // TILE
    specs = [
        pl.BlockSpec((1, HEADS, dq, TILE), lambda b, t: (b, 0, 0, t)),
        pl.BlockSpec((1, HEADS, 1, TILE, dk), lambda b, t: (b, 0, t, 0, 0)),
        pl.BlockSpec((1, HEADS, 1, V_ROWS, TILE), lambda b, t: (b, 0, t, 0, 0)),
    ]
    shapes = [
        jax.ShapeDtypeStruct((batch, HEADS, dq, seq), BF16),
        jax.ShapeDtypeStruct((batch, HEADS, steps, TILE, dk), BF16),
        jax.ShapeDtypeStruct((batch, HEADS, steps, V_ROWS, TILE), BF16),
    ]
    return specs, shapes


def _moba_proj(x, g, w, cos2, sin2, batch, seq):
    tiles = seq // TILE
    out_specs, out_shape = _qkv_out(batch, seq, HEAD_DIM, 2 * HEAD_DIM)
    return pl.pallas_call(
        _moba_proj_kernel,
        grid=(batch, tiles),
        in_specs=[
            pl.BlockSpec((TILE, D_MODEL), lambda b, t: (b * tiles + t, 0)),
            pl.BlockSpec((1, D_MODEL), lambda b, t: (0, 0)),
            pl.BlockSpec((D_MODEL, 3 * HEADS * HEAD_DIM), lambda b, t: (0, 0)),
            pl.BlockSpec((TILE, HEAD_DIM), lambda b, t: (t, 0)),
            pl.BlockSpec((TILE, HEAD_DIM), lambda b, t: (t, 0)),
        ],
        out_specs=out_specs,
        out_shape=out_shape,
        compiler_params=pltpu.CompilerParams(
            dimension_semantics=("arbitrary", "arbitrary"), vmem_limit_bytes=VMEM_LIMIT),
        name="moba_proj",
    )(x, g, w, cos2, sin2)


def _attn_scratch(nh):
    return ([pltpu.VMEM((TILE, Q_TILE), F32)] * (2 * nh) + [pltpu.VMEM((1, Q_TILE), F32)] * (2 * nh)
            + [pltpu.VMEM((1, Q_TILE), F32)] * nh + [pltpu.VMEM((V_ROWS, Q_TILE), F32)] * nh)


def _attention(q_cols, q_next, k_ref, vt_ref, o_ref, scratch, i):
    nh, tq = k_ref.shape[1], o_ref.shape[1]
    assert tq == 2 * TILE
    s_refs = (scratch[:nh], scratch[nh:2 * nh])
    smax_refs = (scratch[2 * nh:3 * nh], scratch[3 * nh:4 * nh])
    m_refs = scratch[4 * nh:5 * nh]
    acc_refs = scratch[5 * nh:6 * nh]
    everything, left, right = slice(0, tq), slice(0, TILE), slice(TILE, tq)

    def produce(h, step, par, cols=everything, q=q_cols):
        s = jnp.dot(k_ref[0, h, step], q(h, cols), preferred_element_type=F32)
        s_refs[par][h][:, cols] = s
        smax_refs[par][h][:, cols] = jnp.max(s, axis=0, keepdims=True)

    def consume(h, step, par, cols=everything, causal=False):
        s = s_refs[par][h][:, cols]
        if causal:
            kpos = lax.broadcasted_iota(jnp.int32, s.shape, 0)
            qpos = lax.broadcasted_iota(jnp.int32, s.shape, 1)
            s = jnp.where(kpos <= qpos, s, NEG)
            s_max = jnp.max(s, axis=0, keepdims=True)
        else:
            s_max = smax_refs[par][h][:, cols]
        m = m_refs[h][:, cols]
        m_new = jnp.maximum(m, s_max)
        m_refs[h][:, cols] = m_new
        p = jnp.exp2(s - m_new).astype(BF16)
        pv = jnp.dot(vt_ref[0, h, step], p, preferred_element_type=F32)
        acc_refs[h][:, cols] = jnp.exp2(m - m_new) * acc_refs[h][:, cols] + pv

    for h in range(nh):
        m_refs[h][...] = jnp.full((1, tq), NEG, F32)
        acc_refs[h][...] = jnp.zeros((V_ROWS, tq), F32)

    @pl.when(i == 0)
    def _():
        for h in range(nh):
            produce(h, 0, 0)

    def run(first, count):
        for c in range(count):
            for h in range(nh):
                produce(h, first + c + 1, 1 - c % 2)
                consume(h, first + c, c % 2)

    def body4(u, _):
        run(4 * u, 4)
        return 0

    def body2(u, _):
        run(4 * (i >> 1), 2)
        return 0

    lax.fori_loop(0, i >> 1, body4, 0)
    lax.fori_loop(0, i & 1, body2, 0)
    for h in range(nh):
        produce(h, 2 * i + 1, 1, right)
        consume(h, 2 * i, 0, left, causal=True)
        consume(h, 2 * i, 0, right)
    for h in range(nh):
        produce(h, 0, 0, q=q_next)
        consume(h, 2 * i + 1, 1, right, causal=True)

    for h in range(nh):
        acc = acc_refs[h][...]
        out = acc[:HEAD_DIM] / acc[HEAD_DIM:HEAD_DIM + 1]
        o_ref[0, :, h * HEAD_DIM:(h + 1) * HEAD_DIM] = out.T.astype(BF16)


def _moba_attn_kernel(qt_ref, qtn_ref, k_ref, vt_ref, o_ref, kmean_ref, qa_ref, *scratch):
    i = pl.program_id(2)
    nh, _, tq = qt_ref.shape[1:]
    steps = k_ref.shape[2]
    nb = TILE // MOBA_BLOCK
    nblk = steps * nb

    nidx = lax.broadcasted_iota(jnp.int32, (nblk, tq), 0)

    def build(h, qt, tile, slot):
        own = (tile * tq + lax.broadcasted_iota(jnp.int32, (nblk, tq), 1)) >> 8
        past = nidx < own
        km = kmean_ref[h]
        km_hi = km.astype(BF16)
        km_lo = (km - km_hi.astype(F32)).astype(BF16)
        gate = (jnp.dot(km_hi, qt, preferred_element_type=F32)
                + jnp.dot(km_lo, qt, preferred_element_type=F32))
        g1 = jnp.where(past, gate, -jnp.inf)
        thr = jnp.max(g1, axis=0, keepdims=True)
        g = g1
        for _ in range(MOBA_TOPK - 1):
            g = jnp.where(g >= thr, -jnp.inf, g)
            thr = jnp.max(g, axis=0, keepdims=True)
        keep = (past & (g1 >= thr)) | (nidx == own)
        qa_ref[slot, h, :HEAD_DIM, :] = qt
        qa_ref[slot, h, HEAD_DIM:HEAD_DIM + nblk, :] = jnp.where(keep, 0.0, NEG).astype(BF16)
        qa_ref[slot, h, HEAD_DIM + nblk:, :] = jnp.zeros((HEAD_DIM - nblk, tq), BF16)

    @pl.when(i == 0)
    def _():
        for h in range(nh):
            for n in range(nblk):
                kb = k_ref[0, h, n // nb, (n % nb) * MOBA_BLOCK:(n % nb + 1) * MOBA_BLOCK, :HEAD_DIM]
                kmean_ref[h, n:n + 1, :] = jnp.mean(kb.astype(F32), axis=0, keepdims=True)
        for h in range(nh):
            build(h, qt_ref[0, h], 0, 0)

    cur, nxt = i & 1, (i + 1) & 1
    for h in range(nh):
        build(h, qtn_ref[0, h], i + 1, nxt)

    _attention(lambda h, cols: qa_ref[cur, h, :, cols], lambda h, cols: qa_ref[nxt, h, :, cols],
               k_ref, vt_ref, o_ref, scratch, i)


def _causal_attn_kernel(qt_ref, qtn_ref, k_ref, vt_ref, o_ref, *scratch):
    _attention(lambda h, cols: qt_ref[0, h, :, cols], lambda h, cols: qtn_ref[0, h, :, cols],
               k_ref, vt_ref, o_ref, scratch, pl.program_id(2))


def _attn_call(kernel_fn, name, qt, k, vt, scratch):
    batch, heads, dq, seq = qt.shape
    dk = k.shape[-1]
    steps = seq // TILE
    n_q = seq // Q_TILE
    nh = ATTN_HEADS
    return pl.pallas_call(
        kernel_fn,
        grid=(batch, heads // nh, n_q),
        in_specs=[
            pl.BlockSpec((1, nh, dq, Q_TILE), lambda b, h, i: (b, h, 0, i)),
            pl.BlockSpec((1, nh, dq, Q_TILE), lambda b, h, i: (b, h, 0, jnp.minimum(i + 1, n_q - 1))),
            pl.BlockSpec((1, nh, steps, TILE, dk), lambda b, h, i: (b, h, 0, 0, 0)),
            pl.BlockSpec((1, nh, steps, V_ROWS, TILE), lambda b, h, i: (b, h, 0, 0, 0)),
        ],
        out_specs=pl.BlockSpec((1, Q_TILE, nh * HEAD_DIM), lambda b, h, i: (b, i, h)),
        out_shape=jax.ShapeDtypeStruct((batch, seq, heads * HEAD_DIM), BF16),
        scratch_shapes=scratch + _attn_scratch(nh),
        compiler_params=pltpu.CompilerParams(
            dimension_semantics=("arbitrary", "arbitrary", "arbitrary"),
            vmem_limit_bytes=VMEM_LIMIT),
        name=name,
    )(qt, qt, k, vt)


def _moba_attn(qt, k, vt):
    seq = qt.shape[3]
    scratch = [pltpu.VMEM((ATTN_HEADS, seq // MOBA_BLOCK, HEAD_DIM), F32),
               pltpu.VMEM((2, ATTN_HEADS, 2 * HEAD_DIM, Q_TILE), BF16)]
    return _attn_call(_moba_attn_kernel, "moba_attn", qt, k, vt, scratch)


def _causal_attn(qt, k, vt):
    return _attn_call(_causal_attn_kernel, "mla_attn", qt, k, vt, [])


def _rope_pairs32(x, cos, sin_lo, sin_hi):
    n = x.shape[1]
    return x * cos + pltpu.roll(x, n - 32, 1) * sin_lo + pltpu.roll(x, 32, 1) * sin_hi


def _mla_proj_kernel(x_ref, g_ref, wa_ref, qn_ref, kvn_ref, wuq_ref, wukv_ref,
                     cos_ref, slo_ref, shi_ref, qt_ref, k_ref, vt_ref):
    hn = _rms(x_ref[...], g_ref[...])
    r = jnp.dot(hn.astype(BF16), wa_ref[...], preferred_element_type=F32)
    cq = _rms(r[:, :MLA_Q_RANK], qn_ref[...])
    ckv = _rms(r[:, MLA_Q_RANK:MLA_Q_RANK + MLA_KV_RANK], kvn_ref[...])
    kr = r[:, MLA_Q_RANK + MLA_KV_RANK:]
    q = jnp.dot(cq.astype(BF16), wuq_ref[...], preferred_element_type=F32)
    kv = jnp.dot(ckv.astype(BF16), wukv_ref[...], preferred_element_type=F32)
    cos = cos_ref[...]
    slo = slo_ref[...]
    shi = shi_ref[...]
    nn = HEADS * MLA_NOPE
    reps = HEADS * MLA_ROPE // 128
    qr = _rope_pairs32(q[:, nn:], jnp.concatenate([cos] * reps, axis=1),
                       jnp.concatenate([slo] * reps, axis=1), jnp.concatenate([shi] * reps, axis=1))
    kr = _rope_pairs32(kr, cos, slo, shi)
    scale = (MLA_NOPE + MLA_ROPE) ** -0.5 * LOG2E
    low_half = lax.broadcasted_iota(jnp.int32, kr.shape, 1) < MLA_ROPE
    for h in range(HEADS):
        pair = qr[:, (h // 2) * 128:(h // 2 + 1) * 128]
        if h % 2:
            pair = pltpu.roll(pair, MLA_ROPE, 1)
        qh = jnp.concatenate(
            [q[:, h * MLA_NOPE:(h + 1) * MLA_NOPE], jnp.where(low_half, pair, 0.0)], axis=1)
        qt_ref[0, h] = (qh * scale).T.astype(BF16)
        kh = jnp.concatenate([kv[:, h * MLA_NOPE:(h + 1) * MLA_NOPE], kr], axis=1).astype(BF16)
        vh = kv[:, nn + h * HEAD_DIM:nn + (h + 1) * HEAD_DIM]
        _store_kv(k_ref, vt_ref, h, kh, vh)


def _mla_proj(x, g, wa, qn, kvn, wuq, wukv, cos, slo, shi, batch, seq):
    tiles = seq // TILE
    full = lambda a: pl.BlockSpec(a.shape, lambda b, t: (0,) * a.ndim)
    tab = pl.BlockSpec((TILE, 128), lambda b, t: (t, 0))
    out_specs, out_shape = _qkv_out(batch, seq, MLA_QK_PAD, MLA_QK_PAD)
    return pl.pallas_call(
        _mla_proj_kernel,
        grid=(batch, tiles),
        in_specs=[
            pl.BlockSpec((TILE, D_MODEL), lambda b, t: (b * tiles + t, 0)),
            full(g), full(wa), full(qn), full(kvn), full(wuq), full(wukv), tab, tab, tab,
        ],
        out_specs=out_specs,
        out_shape=out_shape,
        compiler_params=pltpu.CompilerParams(
            dimension_semantics=("arbitrary", "arbitrary"), vmem_limit_bytes=VMEM_LIMIT),
        name="mla_proj",
    )(x, g, wa, qn, kvn, wuq, wukv, cos, slo, shi)


def _post_attn_kernel(h_ref, o_ref, wo_ref, g_ref, wr_ref, h1_ref, hn_ref, comb_ref, grow_ref):
    h1 = h_ref[...] + jnp.dot(o_ref[...], wo_ref[...], preferred_element_type=F32)
    h1_ref[...] = h1
    hn = _rms(h1, g_ref[...])
    hi = hn.astype(BF16)
    hn_ref[...] = hi
    lo = (hn - hi.astype(F32)).astype(BF16)
    w2 = wr_ref[...]
    r_hi = jnp.dot(hi, w2, preferred_element_type=F32)
    r_lo = jnp.dot(lo, w2, preferred_element_type=F32)
    logits = ((r_hi[:, :ROUTER_LANES] + r_hi[:, ROUTER_LANES:])
              + (r_lo[:, :ROUTER_LANES] + r_lo[:, ROUTER_LANES:]))
    lane_i = lax.broadcasted_iota(jnp.int32, logits.shape, 1)
    lane = lane_i.astype(F32)
    is_group = (lane_i >= N_EXPERTS) & (lane_i < N_EXPERTS + N_GROUPS)

    def first_max(mask):
        v = jnp.max(jnp.where(mask, logits, -jnp.inf), axis=1, keepdims=True)
        idx = jnp.min(jnp.where(mask & (logits == v), lane, 1e9), axis=1, keepdims=True)
        return v, idx

    g_max, g_lane = first_max(is_group)
    g_sum = jnp.sum(jnp.where(is_group, jnp.exp(logits - g_max), 0.0), axis=1, keepdims=True)
    g_w = 1.0 / g_sum
    group_of_lane = (lane_i >> 2).astype(F32)
    in_group = (lane_i < N_EXPERTS) & (group_of_lane == (g_lane - N_EXPERTS))
    v1, i1 = first_max(in_group)
    v2, i2 = first_max(in_group & (lane != i1))
    t = jnp.exp(v2 - v1)
    w1 = g_w / (1.0 + t)
    w2 = w1 * t
    comb_ref[...] = jnp.where(lane == i1, w1, 0.0) + jnp.where(lane == i2, w2, 0.0)
    picked = jnp.where(lane == g_lane, 1.0, 0.0).astype(BF16)
    srow = lax.broadcasted_iota(jnp.int32, (8, ROUTER_LANES), 0)
    slane = lax.broadcasted_iota(jnp.int32, (8, ROUTER_LANES), 1)
    is_g = (srow == 0) & (slane >= N_EXPERTS) & (slane < N_EXPERTS + N_GROUPS)
    sel = jnp.where(is_g, (slane - N_EXPERTS).astype(F32), 0.0).astype(BF16)
    for c in range(grow_ref.shape[0]):
        grow_ref[c] = _nt_dot(sel, picked[c * POST_TILE:(c + 1) * POST_TILE])


def _post_attn(h, o, wo, g, wr):
    n = h.shape[0]
    tile = POST_ATTN_TILE
    row = lambda w: pl.BlockSpec((tile, w), lambda t: (t, 0))
    full = lambda a: pl.BlockSpec(a.shape, lambda t: (0,) * a.ndim)
    return pl.pallas_call(
        _post_attn_kernel,
        grid=(n // tile,),
        in_specs=[row(D_MODEL), row(D_MODEL), full(wo), full(g), full(wr)],
        out_specs=[row(D_MODEL), row(D_MODEL), row(ROUTER_LANES),
                   pl.BlockSpec((tile // POST_TILE, 8, POST_TILE), lambda t: (t, 0, 0))],
        out_shape=[
            jax.ShapeDtypeStruct((n, D_MODEL), F32),
            jax.ShapeDtypeStruct((n, D_MODEL), BF16),
            jax.ShapeDtypeStruct((n, ROUTER_LANES), F32),
            jax.ShapeDtypeStruct((n // POST_TILE, 8, POST_TILE), F32),
        ],
        compiler_params=pltpu.CompilerParams(
            dimension_semantics=("arbitrary",), vmem_limit_bytes=VMEM_LIMIT),
        name="post_attn",
    )(h, o, wo, g, wr)


def _moe_ple_kernel(hn_ref, comb_ref, grow_ref, h1_ref, p_ref, tri_ref, wg_ref, wu_ref, wd_ref,
                    pn_ref, pg_ref, pp_ref, fn_ref, out_ref, xs_ref, cs_ref, ys_ref, *, final):
    t = hn_ref.shape[0]
    gid = grow_ref[0, 0:1, :]
    g_iota = lax.broadcasted_iota(jnp.int32, (8, t), 0).astype(F32)
    member = gid == g_iota
    onehot = jnp.where(member, 1.0, 0.0)
    rank = jnp.dot(onehot.astype(BF16), tri_ref[...], preferred_element_type=F32)
    starts, rows = [], []
    start = jnp.int32(0)
    for g in range(N_GROUPS):
        count = jnp.sum(jnp.where(gid == float(g), 1, 0).astype(jnp.int32))
        padded = ((count + (SORT_ALIGN - 1)) >> 4) << 4
        starts.append(start)
        rows.append(padded)
        start = start + padded
    start_vec = jnp.zeros((8, t), F32)
    for g in range(1, N_GROUPS):
        start_vec = jnp.where(g_iota == float(g), starts[g].astype(F32), start_vec)
    dest = jnp.sum(jnp.where(member, rank + start_vec, 0.0), axis=0, keepdims=True)
    r_iota = lax.broadcasted_iota(jnp.int32, (SORT_ROWS, t), 0).astype(F32)
    perm = jnp.where(r_iota == dest, 1.0, 0.0).astype(BF16)

    comb = comb_ref[...]
    comb_hi = comb.astype(BF16).astype(F32)
    comb_lo = (comb - comb_hi).astype(BF16).astype(F32)
    packed = (comb_hi + pltpu.roll(comb_lo, N_EXPERTS, 1)).astype(BF16)
    both = jnp.dot(perm, jnp.concatenate([hn_ref[...], packed], axis=1),
                   preferred_element_type=F32)
    xs_ref[0:SORT_ROWS, :] = both[:, :D_MODEL].astype(BF16)
    cs_ref[0:SORT_ROWS, :] = both[:, D_MODEL:]
    xs_ref[SORT_ROWS:, :] = jnp.zeros((MOE_CHUNK, D_MODEL), BF16)
    cs_ref[SORT_ROWS:, :] = jnp.zeros((MOE_CHUNK, ROUTER_LANES), F32)
    ys_ref[...] = jnp.zeros(ys_ref.shape, BF16)

    lane = lax.broadcasted_iota(jnp.int32, (MOE_CHUNK, ROUTER_LANES), 1)

    def chunk(g, j):
        off = pl.multiple_of(starts[g] + j * MOE_CHUNK, SORT_ALIGN)
        xc = xs_ref[pl.ds(off, MOE_CHUNK), :]
        cc = cs_ref[pl.ds(off, MOE_CHUNK), :]
        parts = []
        for e in range(g * PER_GROUP, (g + 1) * PER_GROUP):
            a = jnp.dot(xc, wg_ref[e], preferred_element_type=F32)
            u = jnp.dot(xc, wu_ref[e], preferred_element_type=F32)
            c = jnp.sum(jnp.where((lane == e) | (lane == e + N_EXPERTS), cc, 0.0),
                        axis=1, keepdims=True)
            parts.append((a * jax.nn.sigmoid(a) * u * c).astype(BF16))
        y = jnp.dot(jnp.concatenate(parts, axis=1), wd_ref[g], preferred_element_type=F32)
        ys_ref[pl.ds(off, MOE_CHUNK), :] += y.astype(BF16)

    for g in range(N_GROUPS):
        chunk(g, 0)
    for g in range(N_GROUPS):
        n_chunks = sum((rows[g] > k * MOE_CHUNK).astype(jnp.int32)
                       for k in range(SORT_ROWS // MOE_CHUNK))

        def rest(j, _, g=g):
            chunk(g, j)
            return 0

        lax.fori_loop(1, jnp.maximum(n_chunks, 1), rest, 0)

    moe = lax.dot_general(perm, ys_ref[0:SORT_ROWS, :], (((0,), (0,)), ((), ())),
                          preferred_element_type=F32)
    h2 = h1_ref[...] + moe
    gate = jax.nn.sigmoid(
        jnp.dot(_rms(h2, pn_ref[...]).astype(BF16), pg_ref[...], preferred_element_type=F32))
    proj = jnp.dot(p_ref[...].astype(BF16), pp_ref[...], preferred_element_type=F32)
    h3 = h2 + gate * proj
    out_ref[...] = _rms(h3, fn_ref[...]) if final else h3


def _moe_ple(hn, comb, grow, h1, p, wg, wu, wd, pn, pg, pp, fn, final):
    n = hn.shape[0]
    tile = POST_TILE
    row = lambda w: pl.BlockSpec((tile, w), lambda t: (t, 0))
    once = lambda a: pl.BlockSpec(a.shape, lambda t: (0,) * a.ndim, pipeline_mode=pl.Buffered(1))
    tri = jnp.asarray(np.triu(np.ones((tile, tile), np.float32), 1), BF16)
    return pl.pallas_call(
        functools.partial(_moe_ple_kernel, final=final),
        grid=(n // tile,),
        in_specs=[row(D_MODEL), row(ROUTER_LANES), pl.BlockSpec((1, 8, tile), lambda t: (t, 0, 0)),
                  row(D_MODEL), row(PLE_DIM), once(tri), once(wg), once(wu), once(wd),
                  once(pn), once(pg), once(pp), once(fn)],
        out_specs=row(D_MODEL),
        out_shape=jax.ShapeDtypeStruct((n, D_MODEL), F32),
        scratch_shapes=[pltpu.VMEM((SORT_ROWS + MOE_CHUNK, D_MODEL), BF16),
                        pltpu.VMEM((SORT_ROWS + MOE_CHUNK, ROUTER_LANES), F32),
                        pltpu.VMEM((SORT_ROWS + MOE_CHUNK, D_MODEL), BF16)],
        compiler_params=pltpu.CompilerParams(
            dimension_semantics=("arbitrary",), vmem_limit_bytes=VMEM_LIMIT),
        name="moe_ple",
    )(hn, comb, grow, h1, p, tri, wg, wu, wd, pn, pg, pp, fn)


def _rope_tables(seq, half):
    inv = THETA ** (-np.arange(half, dtype=np.float64) / half)
    ang = np.arange(seq, dtype=np.float64)[:, None] * inv[None, :]
    return np.cos(ang).astype(np.float32), np.sin(ang).astype(np.float32)


def _split_bf16(w):
    hi = w.astype(BF16)
    return hi, (w - hi.astype(F32)).astype(BF16)


def kernel(x, p, attn_norm, ffn_norm, ple_norm, final_norm, moba_wqkv, moba_wo, mla_wdq, mla_qnorm,
           mla_wuq, mla_wdkv, mla_kvnorm, mla_wukv, mla_wo, moe_wgroup, moe_wexpert, moe_wgate,
           moe_wup, moe_wdown, ple_gate, ple_proj):
    batch, seq, d = x.shape
    n = batch * seq
    h = x.reshape(n, d)

    cos64, sin64 = _rope_tables(seq, HEAD_DIM // 2)
    moba_cos = jnp.asarray(np.concatenate([cos64, cos64], axis=1))
    moba_sin = jnp.asarray(np.concatenate([-sin64, sin64], axis=1))
    cos32, sin32 = _rope_tables(seq, MLA_ROPE // 2)
    z32 = np.zeros_like(sin32)
    mla_cos = jnp.asarray(np.concatenate([cos32] * 4, axis=1))
    mla_slo = jnp.asarray(np.concatenate([-sin32, z32, -sin32, z32], axis=1))
    mla_shi = jnp.asarray(np.concatenate([z32, sin32, z32, sin32], axis=1))

    row = lambda v: v.reshape(1, -1)

    for layer in range(N_LAYERS):
        j = layer // 2
        if layer % 2 == 0:
            q, k, vt = _moba_proj(h, row(attn_norm[layer]), moba_wqkv[j].astype(BF16),
                                  moba_cos, moba_sin, batch, seq)
            o = _moba_attn(q, k, vt)
            wo = moba_wo[j]
        else:
            wuq = mla_wuq[j].reshape(MLA_Q_RANK, HEADS, MLA_NOPE + MLA_ROPE)
            wuq = jnp.concatenate([wuq[:, :, :MLA_NOPE].reshape(MLA_Q_RANK, -1),
                                   wuq[:, :, MLA_NOPE:].reshape(MLA_Q_RANK, -1)], axis=1)
            wukv = mla_wukv[j].reshape(MLA_KV_RANK, HEADS, MLA_NOPE + HEAD_DIM)
            wukv = jnp.concatenate([wukv[:, :, :MLA_NOPE].reshape(MLA_KV_RANK, -1),
                                    wukv[:, :, MLA_NOPE:].reshape(MLA_KV_RANK, -1)], axis=1)
            wa = jnp.concatenate(
                [mla_wdq[j], mla_wdkv[j], jnp.zeros((d, 128 - MLA_ROPE), F32)], axis=1)
            q, k, vt = _mla_proj(h, row(attn_norm[layer]), wa.astype(BF16), row(mla_qnorm[j]),
                                 row(mla_kvnorm[j]), wuq.astype(BF16), wukv.astype(BF16),
                                 mla_cos, mla_slo, mla_shi, batch, seq)
            o = _causal_attn(q, k, vt)
            wo = mla_wo[j]

        wr = jnp.concatenate(
            [moe_wexpert[layer], moe_wgroup[layer],
             jnp.zeros((d, ROUTER_LANES - N_EXPERTS - N_GROUPS), F32)], axis=1)
        h1, hn, comb, grow = _post_attn(h, o.reshape(n, -1), wo.astype(BF16), row(ffn_norm[layer]),
                                        jnp.concatenate(_split_bf16(wr), axis=1))

        h = _moe_ple(hn, comb, grow, h1, p[layer].reshape(n, PLE_DIM), moe_wgate[layer].astype(BF16),
                     moe_wup[layer].astype(BF16),
                     moe_wdown[layer].reshape(N_GROUPS, PER_GROUP * EXPERT_FF, d).astype(BF16),
                     row(ple_norm[layer]), ple_gate[layer].astype(BF16),
                     ple_proj[layer].astype(BF16), row(final_norm), layer == N_LAYERS - 1)

    return h.reshape(batch, seq, d)
```
